```python
import jax, jax.numpy as jnp
from jax import lax
import numpy as np

D_MODEL = 1024
BATCH = 8
SEQ = 8192
DEPTH = 2

D_CONV = D_MODEL
CONV_WIDTH = 31
D_POOL = D_MODEL
POOL_WINDOWS = (2, 4, 8, 16)
N_POOL_GROUPS = len(POOL_WINDOWS)
POOL_GROUP = D_POOL // N_POOL_GROUPS
D_SGU = D_MODEL
SGU_CHUNK = 128
N_SGU_HEADS = 8
SGU_HEAD = D_SGU // N_SGU_HEADS
N_BRANCHES = 3
OFF_A = 2 * D_CONV
OFF_B = OFF_A + D_POOL
OFF_C = OFF_B + 2 * D_SGU
D_IN = OFF_C + N_BRANCHES * D_MODEL
D_FF = 2816
N_EXPERTS = 8
TOP_K = 2
D_EXPERT = 3584
MOE_BLOCK = 512
N_DENSE = (DEPTH + 1) // 2
N_MOE = DEPTH // 2
N_MOD = 6
EPS = 1e-6

kernel_name = "hybrid_conv_pool_sgu_moe_adaln"


def rmsnorm(x, g):
    xf = x.astype(jnp.float32)
    y = xf * lax.rsqrt(jnp.mean(xf * xf, axis=-1, keepdims=True) + EPS)
    return (y * g.astype(jnp.float32)).astype(x.dtype)


def layernorm(x, g, b):
    xf = x.astype(jnp.float32)
    mu = jnp.mean(xf, axis=-1, keepdims=True)
    var = jnp.mean(jnp.square(xf - mu), axis=-1, keepdims=True)
    y = (xf - mu) * lax.rsqrt(var + EPS)
    return (y * g.astype(jnp.float32) + b.astype(jnp.float32)).astype(x.dtype)


def conformer_conv(a_in, conv_w, conv_b, ln_g, ln_b, pw_w, pw_b):
    a, g = jnp.split(a_in, 2, axis=-1)
    u = a * jax.nn.sigmoid(g)
    u = lax.conv_general_dilated(
        u, conv_w[:, None, :], window_strides=(1,), padding=[(CONV_WIDTH - 1, 0)],
        dimension_numbers=("NWC", "WIO", "NWC"), feature_group_count=D_CONV) + conv_b
    u = jax.nn.silu(layernorm(u, ln_g, ln_b))
    return u @ pw_w + pw_b


def multiscale_pool(p, pool_w, pool_b, pool_scale):
    B, S, _ = p.shape
    pf = p.astype(jnp.float32)
    P = jnp.cumsum(pf, axis=1)
    wmax = POOL_WINDOWS[-1]
    P_pad = jnp.concatenate([jnp.zeros((B, wmax, D_POOL), jnp.float32), P], axis=1)
    pos1 = jnp.arange(1, S + 1, dtype=jnp.float32)[None, :, None]
    outs = []
    for gi, w in enumerate(POOL_WINDOWS):
        sl = slice(gi * POOL_GROUP, (gi + 1) * POOL_GROUP)
        win_sum = P[..., sl] - P_pad[:, wmax - w:wmax - w + S, sl]
        outs.append(win_sum / jnp.minimum(pos1, float(w)) - pf[..., sl])
    d = jnp.stack(outs, axis=2).astype(p.dtype)
    y = jnp.einsum("bsgc,gce->bsge", d, pool_w).reshape(B, S, D_POOL) + pool_b
    return y * pool_scale


def spatial_gating(s_in, ln_g, ln_b, sgu_w, sgu_b):
    B, S, _ = s_in.shape
    u, v = jnp.split(jax.nn.gelu(s_in), 2, axis=-1)
    v = layernorm(v, ln_g, ln_b)
    v = v.reshape(B, S // SGU_CHUNK, SGU_CHUNK, N_SGU_HEADS, SGU_HEAD)
    mask = jnp.tril(jnp.ones((SGU_CHUNK, SGU_CHUNK), dtype=bool))
    w = jnp.where(mask[None], sgu_w, jnp.zeros_like(sgu_w))
    mixed = jnp.einsum("hts,bnshc->bnthc", w, v) + jnp.transpose(sgu_b)[:, :, None]
    return u * mixed.reshape(B, S, D_SGU)


def mixer_sublayer(h, w_in, b_in, conv_w, conv_b, conv_ln_g, conv_ln_b, conv_pw_w, conv_pw_b,
                   pool_w, pool_b, pool_scale, sgu_ln_g, sgu_ln_b, sgu_w, sgu_b, w_out):
    B, S, _ = h.shape
    z = jnp.einsum("bsd,de->bse", h, w_in) + b_in
    a_in, p_in, s_in, g_in = jnp.split(z, [OFF_A, OFF_B, OFF_C], axis=-1)
    y_a = conformer_conv(a_in, conv_w, conv_b, conv_ln_g, conv_ln_b, conv_pw_w, conv_pw_b)
    y_b = multiscale_pool(p_in, pool_w, pool_b, pool_scale)
    y_c = spatial_gating(s_in, sgu_ln_g, sgu_ln_b, sgu_w, sgu_b)
    g = jax.nn.sigmoid(g_in).reshape(B, S, N_BRANCHES, D_MODEL)
    merged = g[:, :, 0] * y_a + g[:, :, 1] * y_b + g[:, :, 2] * y_c
    return merged @ w_out


def swiglu(h, w1, w3, w2):
    return (jax.nn.silu(h @ w1) * (h @ w3)) @ w2


def moe_swiglu(h, router_w, w1, w3, w2):
    B, S, D = h.shape
    T = B * S
    ht = h.reshape(T, D)
    logits = ht.astype(jnp.float32) @ router_w.astype(jnp.float32)
    top_vals, top_idx = lax.top_k(logits, TOP_K)
    gates = jax.nn.softmax(top_vals, axis=-1).astype(h.dtype)
    flat_e = top_idx.reshape(-1).astype(jnp.int32)
    flat_tok = jnp.repeat(jnp.arange(T, dtype=jnp.int32), TOP_K)
    flat_gate = gates.reshape(-1)
    order = jnp.argsort(flat_e, stable=True)
    sorted_e = flat_e[order]
    counts = jnp.bincount(flat_e, length=N_EXPERTS).astype(jnp.int32)
    padded = (counts + MOE_BLOCK - 1) // MOE_BLOCK * MOE_BLOCK
    start_sorted = jnp.cumsum(counts) - counts
    ends_padded = jnp.cumsum(padded)
    start_padded = ends_padded - padded
    rank = jnp.arange(T * TOP_K, dtype=jnp.int32) - start_sorted[sorted_e]
    dest = start_padded[sorted_e] + rank
    n_slots = -(-(T * TOP_K + N_EXPERTS * (MOE_BLOCK - 1)) // MOE_BLOCK) * MOE_BLOCK
    n_blocks = n_slots // MOE_BLOCK
    slot_tok = jnp.full((n_slots,), T, jnp.int32).at[dest].set(flat_tok[order])
    slot_gate = jnp.zeros((n_slots,), h.dtype).at[dest].set(flat_gate[order])
    block_start = jnp.arange(n_blocks, dtype=jnp.int32) * MOE_BLOCK
    block_expert = jnp.minimum(jnp.searchsorted(ends_padded, block_start, side="right"),
                               N_EXPERTS - 1).astype(jnp.int32)
    h_pad = jnp.concatenate([ht, jnp.zeros((1, D), h.dtype)], axis=0)
    xs = h_pad[slot_tok].reshape(n_blocks, MOE_BLOCK, D)

    def expert_block(args):
        xb, e = args
        return swiglu(xb, w1[e], w3[e], w2[e])

    ys = lax.map(expert_block, (xs, block_expert)).reshape(n_slots, D)
    out = jnp.zeros((T + 1, D), ys.dtype).at[slot_tok].add(ys * slot_gate[:, None])
    return out[:T].reshape(B, S, D)


def setup_inputs(seed: int = 0) -> dict:
    key = jax.random.key(seed)
    ks = jax.random.split(key, 32)
    L, D = DEPTH, D_MODEL
    f32 = jnp.float32

    def nrm(k, shape, scale):
        return jax.random.normal(k, shape, f32) * scale

    return {
        "x": nrm(ks[0], (BATCH, SEQ, D), 1.0),
        "c": nrm(ks[1], (BATCH, D), 1.0),
        "ada_w": nrm(ks[2], (L, D, N_MOD * D), 0.5 * D ** -0.5),
        "ada_b": nrm(ks[3], (L, N_MOD * D), 0.02),
        "norm1_g": 1.0 + nrm(ks[4], (L, D), 0.05),
        "norm2_g": 1.0 + nrm(ks[5], (L, D), 0.05),
        "w_in": nrm(ks[6], (L, D, D_IN), D ** -0.5),
        "b_in": nrm(ks[7], (L, D_IN), 0.02),
        "conv_w": nrm(ks[8], (L, CONV_WIDTH, D_CONV), CONV_WIDTH ** -0.5),
        "conv_b": nrm(ks[9], (L, D_CONV), 0.02),
        "conv_ln_g": 1.0 + nrm(ks[10], (L, D_CONV), 0.05),
        "conv_ln_b": nrm(ks[11], (L, D_CONV), 0.02),
        "conv_pw_w": nrm(ks[12], (L, D_CONV, D), D_CONV ** -0.5),
        "conv_pw_b": nrm(ks[13], (L, D), 0.02),
        "pool_w": nrm(ks[14], (L, N_POOL_GROUPS, POOL_GROUP, POOL_GROUP), POOL_GROUP ** -0.5),
        "pool_b": nrm(ks[15], (L, D_POOL), 0.02),
        "pool_scale": 1.0 + nrm(ks[16], (L, D_POOL), 0.1),
        "sgu_ln_g": 1.0 + nrm(ks[17], (L, D_SGU), 0.05),
        "sgu_ln_b": nrm(ks[18], (L, D_SGU), 0.02),
        "sgu_w": nrm(ks[19], (L, N_SGU_HEADS, SGU_CHUNK, SGU_CHUNK), 0.5 * SGU_CHUNK ** -0.5),
        "sgu_b": 1.0 + nrm(ks[20], (L, N_SGU_HEADS, SGU_CHUNK), 0.05),
        "w_out": nrm(ks[21], (L, D, D), 0.5 * D ** -0.5),
        "ffn_w1": nrm(ks[22], (N_DENSE, D, D_FF), D ** -0.5),
        "ffn_w3": nrm(ks[23], (N_DENSE, D, D_FF), D ** -0.5),
        "ffn_w2": nrm(ks[24], (N_DENSE, D_FF, D), D_FF ** -0.5),
        "router_w": nrm(ks[25], (N_MOE, D, N_EXPERTS), D ** -0.5),
        "moe_w1": nrm(ks[26], (N_MOE, N_EXPERTS, D, D_EXPERT), D ** -0.5),
        "moe_w3": nrm(ks[27], (N_MOE, N_EXPERTS, D, D_EXPERT), D ** -0.5),
        "moe_w2": nrm(ks[28], (N_MOE, N_EXPERTS, D_EXPERT, D), D_EXPERT ** -0.5),
        "final_norm_g": 1.0 + nrm(ks[29], (D,), 0.05),
    }


def reference(x, c, ada_w, ada_b, norm1_g, norm2_g, w_in, b_in, conv_w, conv_b, conv_ln_g,
              conv_ln_b, conv_pw_w, conv_pw_b, pool_w, pool_b, pool_scale, sgu_ln_g, sgu_ln_b,
              sgu_w, sgu_b, w_out, ffn_w1, ffn_w3, ffn_w2, router_w, moe_w1, moe_w3, moe_w2,
              final_norm_g):
    c_act = jax.nn.silu(c)
    for i in range(DEPTH):
        mod = (c_act @ ada_w[i] + ada_b[i])[:, None, :]
        sh1, sc1, gt1, sh2, sc2, gt2 = jnp.split(mod, N_MOD, axis=-1)
        h = rmsnorm(x, norm1_g[i]) * (1.0 + sc1) + sh1
        y = mixer_sublayer(h, w_in[i], b_in[i], conv_w[i], conv_b[i], conv_ln_g[i], conv_ln_b[i],
                           conv_pw_w[i], conv_pw_b[i], pool_w[i], pool_b[i], pool_scale[i],
                           sgu_ln_g[i], sgu_ln_b[i], sgu_w[i], sgu_b[i], w_out[i])
        x = x + gt1 * y
        h = rmsnorm(x, norm2_g[i]) * (1.0 + sc2) + sh2
        j = i // 2
        if i % 2 == 0:
            f = swiglu(h, ffn_w1[j], ffn_w3[j], ffn_w2[j])
        else:
            f = moe_swiglu(h, router_w[j], moe_w1[j], moe_w3[j], moe_w2[j])
        x = x + gt2 * f
    return rmsnorm(x, final_norm_g)
```

```python
import functools
import math

import jax
import jax.numpy as jnp
from jax import lax
from jax.experimental import pallas as pl
from jax.experimental.pallas import tpu as pltpu

F32 = jnp.float32
BF16 = jnp.bfloat16

D_MODEL = 1024
CONV_WIDTH = 31
POOL_WINDOWS = (2, 4, 8, 16)
POOL_GROUP = D_MODEL // len(POOL_WINDOWS)
SGU_CHUNK = 128
N_SGU_HEADS = 8
N_EXPERTS = 8
TOP_K = 2
N_MOD = 6
EPS = 1e-6

LANES = 128
SUBLANES = 8
ROW_TILES = D_MODEL // LANES

CONV_HALO = 32
POOL_HALO = 16

VMEM_LIMIT = 56 * 1024 * 1024

TM_IN = 512
TS_MIX = 256
TM_FFN = 512
TF_FFN = 1408
TM_ROUTE = 512
BM_MOE = 1024
TF_MOE = 896
TM_OUT = 512


def _sigmoid(x):
    return 0.5 * (1.0 + jnp.tanh(0.5 * x))


def _silu(x):
    return x * _sigmoid(x)


def _gelu_tanh(x):
    c = math.sqrt(2.0 / math.pi)
    return x * (0.5 * (1.0 + jnp.tanh(c * (x + 0.044715 * (x * x * x)))))


def _dot(a, b):
    return jnp.dot(a, b, preferred_element_type=F32)


def _layernorm(x, g, b):
    mu = jnp.mean(x, axis=-1, keepdims=True)
    xc = x - mu
    var = jnp.mean(xc * xc, axis=-1, keepdims=True)
    return xc * lax.rsqrt(var + EPS) * g + b


def _ada_rmsnorm(x, g, scale, shift):
    y = x * lax.rsqrt(jnp.mean(x * x, axis=-1, keepdims=True) + EPS)
    return (y * g) * (1.0 + scale) + shift


def _params(*sem):
    return pltpu.CompilerParams(dimension_semantics=sem, vmem_limit_bytes=VMEM_LIMIT)


def _resident(shape):
    nd = len(shape)
    return pl.BlockSpec(shape, lambda *_: (0,) * nd, pipeline_mode=pl.Buffered(1))


def _ada_kernel(c_ref, w_ref, b_ref, o_ref):
    ca = _silu(c_ref[...]).astype(BF16)
    o_ref[...] = _dot(ca, w_ref[...].astype(BF16)) + b_ref[...]


def _ada_mod(c, ada_w, ada_b):
    L, D, N = ada_w.shape
    B = c.shape[0]
    out = pl.pallas_call(
        _ada_kernel,
        grid=(L, N // D),
        in_specs=[
            pl.BlockSpec((B, D), lambda l, j: (0, 0)),
            pl.BlockSpec((None, D, D), lambda l, j: (l, 0, j)),
            pl.BlockSpec((None, 1, D), lambda l, j: (l, 0, j)),
        ],
        out_specs=pl.BlockSpec((None, B, D), lambda l, j: (l, 0, j)),
        out_shape=jax.ShapeDtypeStruct((L, B, N), F32),
        compiler_params=_params("arbitrary", "arbitrary"),
        name="ada_mod",
    )(c, ada_w, ada_b.reshape(L, 1, N))
    return out.reshape(L, B, N_MOD, D)


def _mixer_in_kernel(x_ref, mod_ref, ng_ref, w_ref, b_ref, lng_ref, lnb_ref,
                     ua_ref, p_ref, su_ref, sv_ref, g0_ref, g1_ref, g2_ref):
    D = D_MODEL
    h = _ada_rmsnorm(x_ref[...], ng_ref[...], mod_ref[0, 1:2, :], mod_ref[0, 0:1, :])
    hb = h.astype(BF16)

    def proj(j):
        return _dot(hb, w_ref[:, j * D:(j + 1) * D]) + b_ref[:, j * D:(j + 1) * D]

    ua_ref[...] = (proj(0) * _sigmoid(proj(1))).astype(BF16)
    p_ref[...] = proj(2).astype(BF16)
    su_ref[...] = _gelu_tanh(proj(3)).astype(BF16)
    sv_ref[...] = _layernorm(_gelu_tanh(proj(4)), lng_ref[...], lnb_ref[...]).astype(BF16)
    g0_ref[...] = _sigmoid(proj(5)).astype(BF16)
    g1_ref[...] = _sigmoid(proj(6)).astype(BF16)
    g2_ref[...] = _sigmoid(proj(7)).astype(BF16)


def _mixer_in(x2, mod, norm_g, w_in, b_in, sgu_ln_g, sgu_ln_b, seq):
    T, D = x2.shape
    tm = TM_IN
    tiles_per_seq = seq // tm
    row = pl.BlockSpec((tm, D), lambda i: (i, 0))
    outs = pl.pallas_call(
        _mixer_in_kernel,
        grid=(T // tm,),
        in_specs=[
            row,
            pl.BlockSpec((1, N_MOD, D), lambda i: (i // tiles_per_seq, 0, 0)),
            _resident((1, D)),
            _resident(w_in.shape),
            _resident((1, w_in.shape[1])),
            _resident((1, D)),
            _resident((1, D)),
        ],
        out_specs=[row] * 7,
        out_shape=[jax.ShapeDtypeStruct((T, D), BF16)] * 7,
        compiler_params=_params("arbitrary"),
        name="mixer_in",
    )(x2, mod, norm_g.reshape(1, D), w_in, b_in.reshape(1, -1),
      sgu_ln_g.reshape(1, D), sgu_ln_b.reshape(1, D))
    return outs


def _mixer_mid_kernel(x_ref, mod_ref, ua_ref, p_ref, su_ref, sv_ref, g0_ref, g1_ref, g2_ref,
                      cw_ref, cb_ref, clg_ref, clb_ref, pw_ref, pb_ref,
                      plw_ref, plb_ref, pls_ref, sw_ref, sbt_ref, wo_ref,
                      o_ref, u_ext, p_ext, cv_ref, mg_ref):
    D = D_MODEL
    ts = TS_MIX
    s_idx = pl.program_id(1)

    @pl.when(s_idx == 0)
    def _():
        u_ext[0:CONV_HALO, :] = jnp.zeros((CONV_HALO, D), F32)
        p_ext[0:POOL_HALO, :] = jnp.zeros((POOL_HALO, D), F32)

    @pl.when(s_idx > 0)
    def _():
        u_ext[0:CONV_HALO, :] = u_ext[ts:ts + CONV_HALO, :]
        p_ext[0:POOL_HALO, :] = p_ext[ts:ts + POOL_HALO, :]

    u_ext[CONV_HALO:CONV_HALO + ts, :] = ua_ref[...].astype(F32)
    p_ext[POOL_HALO:POOL_HALO + ts, :] = p_ref[...].astype(F32)

    ri = lax.broadcasted_iota(jnp.int32, (SGU_CHUNK, SGU_CHUNK), 0)
    ci = lax.broadcasted_iota(jnp.int32, (SGU_CHUNK, SGU_CHUNK), 1)
    causal = ri >= ci
    for hd in range(N_SGU_HEADS):
        cols = slice(hd * SGU_CHUNK, (hd + 1) * SGU_CHUNK)
        wm = jnp.where(causal, sw_ref[hd], 0.0).astype(BF16)
        bcol = sbt_ref[:, hd:hd + 1]
        for n in range(ts // SGU_CHUNK):
            rows = slice(n * SGU_CHUNK, (n + 1) * SGU_CHUNK)
            mixed = _dot(wm, sv_ref[rows, cols]) + bcol
            mg_ref[rows, cols] = (g2_ref[rows, cols].astype(F32)
                                  * su_ref[rows, cols].astype(F32) * mixed)

    pos1 = (s_idx * ts + lax.broadcasted_iota(jnp.int32, (ts, 1), 0) + 1).astype(F32)
    for gi, w in enumerate(POOL_WINDOWS):
        cols = slice(gi * POOL_GROUP, (gi + 1) * POOL_GROUP)
        cur = p_ext[POOL_HALO:POOL_HALO + ts, cols]
        win = cur
        for j in range(1, w):
            win = win + p_ext[POOL_HALO - j:POOL_HALO - j + ts, cols]
        d = win * (1.0 / jnp.minimum(pos1, float(w))) - cur
        yb = (_dot(d.astype(BF16), plw_ref[gi]) + plb_ref[:, cols]) * pls_ref[:, cols]
        mg_ref[:, cols] += g1_ref[:, cols].astype(F32) * yb

    rc = 32
    for r in range(ts // rc):
        acc = jnp.zeros((rc, D), F32) + cb_ref[...]
        base = r * rc + CONV_HALO - (CONV_WIDTH - 1)
        for k in range(CONV_WIDTH):
            acc = acc + u_ext[base + k:base + k + rc, :] * cw_ref[k:k + 1, :]
        cv_ref[r * rc:(r + 1) * rc, :] = acc
    cn = _silu(_layernorm(cv_ref[...], clg_ref[...], clb_ref[...])).astype(BF16)
    ya = _dot(cn, pw_ref[...]) + pb_ref[...]
    merged = mg_ref[...] + g0_ref[...].astype(F32) * ya

    y = _dot(merged.astype(BF16), wo_ref[...])
    o_ref[...] = x_ref[...] + mod_ref[0, 2:3, :] * y


def _mixer_mid(x2, mod, parts, conv_w, conv_b, conv_ln_g, conv_ln_b, conv_pw_w, conv_pw_b,
               pool_w, pool_b, pool_scale, sgu_w, sgu_b, w_out, batch, seq):
    T, D = x2.shape
    ts = TS_MIX
    ns = seq // ts
    row = pl.BlockSpec((ts, D), lambda b, s: (b * ns + s, 0))
    vec = lambda a: a.reshape(1, D)
    return pl.pallas_call(
        _mixer_mid_kernel,
        grid=(batch, ns),
        in_specs=[
            row,
            pl.BlockSpec((1, N_MOD, D), lambda b, s: (b, 0, 0)),
        ] + [row] * 7 + [
            _resident((CONV_WIDTH, D)), _resident((1, D)), _resident((1, D)), _resident((1, D)),
            _resident((D, D)), _resident((1, D)),
            _resident(pool_w.shape), _resident((1, D)), _resident((1, D)),
            _resident(sgu_w.shape), _resident((SGU_CHUNK, N_SGU_HEADS)),
            _resident((D, D)),
        ],
        out_specs=row,
        out_shape=jax.ShapeDtypeStruct((T, D), F32),
        scratch_shapes=[
            pltpu.VMEM((ts + CONV_HALO, D), F32),
            pltpu.VMEM((ts + POOL_HALO, D), F32),
            pltpu.VMEM((ts, D), F32),
            pltpu.VMEM((ts, D), F32),
        ],
        compiler_params=_params("arbitrary", "arbitrary"),
        name="mixer_mid",
    )(x2, mod, *parts, conv_w, vec(conv_b), vec(conv_ln_g), vec(conv_ln_b), conv_pw_w,
      vec(conv_pw_b), pool_w, vec(pool_b), vec(pool_scale), sgu_w, sgu_b.T, w_out)


def _ffn_kernel(x_ref, mod_ref, ng_ref, w1_ref, w3_ref, w2_ref, o_ref, h_ref, acc_ref):
    f = pl.program_id(1)

    @pl.when(f == 0)
    def _():
        h = _ada_rmsnorm(x_ref[...], ng_ref[...], mod_ref[0, 4:5, :], mod_ref[0, 3:4, :])
        h_ref[...] = h.astype(BF16)
        acc_ref[...] = jnp.zeros_like(acc_ref)

    hb = h_ref[...]
    act = (_silu(_dot(hb, w1_ref[...])) * _dot(hb, w3_ref[...])).astype(BF16)
    acc_ref[...] += _dot(act, w2_ref[...])

    @pl.when(f == pl.num_programs(1) - 1)
    def _():
        o_ref[...] = x_ref[...] + mod_ref[0, 5:6, :] * acc_ref[...]


def _ffn(x2, mod, norm_g, w1, w3, w2, seq):
    T, D = x2.shape
    F = w1.shape[1]
    tm, tf = TM_FFN, TF_FFN
    tiles_per_seq = seq // tm
    row = pl.BlockSpec((tm, D), lambda i, f: (i, 0))
    return pl.pallas_call(
        _ffn_kernel,
        grid=(T // tm, F // tf),
        in_specs=[
            row,
            pl.BlockSpec((1, N_MOD, D), lambda i, f: (i // tiles_per_seq, 0, 0)),
            _resident((1, D)),
            pl.BlockSpec((D, tf), lambda i, f: (0, f)),
            pl.BlockSpec((D, tf), lambda i, f: (0, f)),
            pl.BlockSpec((tf, D), lambda i, f: (f, 0)),
        ],
        out_specs=row,
        out_shape=jax.ShapeDtypeStruct((T, D), F32),
        scratch_shapes=[pltpu.VMEM((tm, D), BF16), pltpu.VMEM((tm, D), F32)],
        compiler_params=_params("arbitrary", "arbitrary"),
        name="ffn",
    )(x2, mod, norm_g.reshape(1, D), w1, w3, w2)


def _split_bf16(v):
    hi = v.astype(BF16)
    lo = (v - hi.astype(F32)).astype(BF16)
    return hi, lo


def _router_kernel(x_ref, mod_ref, ng_ref, rw_ref, h_ref, idx_ref, gate_ref):
    tm = x_ref.shape[0]
    h = _ada_rmsnorm(x_ref[...], ng_ref[...], mod_ref[0, 4:5, :], mod_ref[0, 3:4, :])
    for s in range(ROW_TILES):
        h_ref[:, s, :] = h[:, s * LANES:(s + 1) * LANES]

    h_hi, h_lo = _split_bf16(h)
    w_hi, w_lo = _split_bf16(rw_ref[...])
    logits = _dot(h_hi, w_hi) + (_dot(h_lo, w_hi) + _dot(h_hi, w_lo))

    lane = lax.broadcasted_iota(jnp.int32, (tm, LANES), 1)
    neg = jnp.float32(-jnp.inf)
    logits = jnp.where(lane < N_EXPERTS, logits, neg)
    m1 = jnp.max(logits, axis=-1, keepdims=True)
    i1 = jnp.min(jnp.where(logits == m1, lane, LANES), axis=-1, keepdims=True)
    rest = jnp.where(lane == i1, neg, logits)
    m2 = jnp.max(rest, axis=-1, keepdims=True)
    i2 = jnp.min(jnp.where(rest == m2, lane, LANES), axis=-1, keepdims=True)
    e2 = jnp.exp(m2 - m1)
    den = 1.0 + e2
    idx_ref[...] = jnp.where(lane == 0, i1, i2)[:, :TOP_K]
    gate_ref[...] = jnp.where(lane == 0, 1.0 / den, e2 / den)[:, :TOP_K]


def _router(x2, mod, norm_g, router_w, seq):
    T, D = x2.shape
    tm = TM_ROUTE
    tiles_per_seq = seq // tm
    rw = jnp.zeros((D, LANES), F32).at[:, :N_EXPERTS].set(router_w)
    return pl.pallas_call(
        _router_kernel,
        grid=(T // tm,),
        in_specs=[
            pl.BlockSpec((tm, D), lambda i: (i, 0)),
            pl.BlockSpec((1, N_MOD, D), lambda i: (i // tiles_per_seq, 0, 0)),
            _resident((1, D)),
            _resident((D, LANES)),
        ],
        out_specs=[
            pl.BlockSpec((tm, ROW_TILES, LANES), lambda i: (i, 0, 0)),
            pl.BlockSpec((tm, TOP_K), lambda i: (i, 0)),
            pl.BlockSpec((tm, TOP_K), lambda i: (i, 0)),
        ],
        out_shape=[
            jax.ShapeDtypeStruct((T, ROW_TILES, LANES), F32),
            jax.ShapeDtypeStruct((T, TOP_K), jnp.int32),
            jax.ShapeDtypeStruct((T, TOP_K), F32),
        ],
        compiler_params=_params("arbitrary"),
        name="router",
    )(x2, mod, norm_g.reshape(1, D), rw)


def _dispatch_plan(top_idx, n_tokens):
    bm = BM_MOE
    n_flat = n_tokens * TOP_K
    n_blocks = -(-(n_flat + N_EXPERTS * (bm - 1)) // bm)
    n_slots = n_blocks * bm
    flat_e = top_idx.reshape(-1)
    order = jnp.argsort(flat_e, stable=True).astype(jnp.int32)
    counts = jnp.sum(flat_e[:, None] == jnp.arange(N_EXPERTS, dtype=jnp.int32)[None, :],
                     axis=0, dtype=jnp.int32)
    padded = (counts + bm - 1) // bm * bm
    start_sorted = jnp.cumsum(counts) - counts
    ends_padded = jnp.cumsum(padded)
    start_padded = ends_padded - padded
    n_used = (ends_padded[-1] // bm).astype(jnp.int32)
    block_start = jnp.arange(n_blocks, dtype=jnp.int32) * bm
    block_expert = jnp.minimum(jnp.searchsorted(ends_padded, block_start, side="right"),
                               N_EXPERTS - 1).astype(jnp.int32)
    last_e = block_expert[jnp.maximum(n_used - 1, 0)]
    block_expert = jnp.where(jnp.arange(n_blocks) < n_used, block_expert, last_e)
    slot = jnp.arange(n_slots, dtype=jnp.int32)
    slot_e = jnp.repeat(block_expert, bm)
    rank = slot - start_padded[slot_e]
    valid = (rank < counts[slot_e]) & (slot < ends_padded[-1])
    flat = order[jnp.clip(start_sorted[slot_e] + rank, 0, n_flat - 1)]
    spare = n_flat + jnp.cumsum(jnp.logical_not(valid).astype(jnp.int32)) - 1
    slot_src = jnp.where(valid, flat // TOP_K, 0).astype(jnp.int32)
    slot_dst = jnp.where(valid, flat, spare).astype(jnp.int32)
    return (slot_src.reshape(n_blocks, 1, bm), slot_dst.reshape(n_blocks, 1, bm),
            block_expert, n_used.reshape(1), n_blocks)


def _moe_kernel(be_ref, nu_ref, src_cur, src_nxt, dst_cur, h_hbm, w1_ref, w3_ref, w2_ref,
                y_hbm, xbuf, xs_ref, acc_ref, ybuf, gsem, ssem):
    bm = BM_MOE
    b = pl.program_id(0)
    f = pl.program_id(1)
    n_used = nu_ref[0]
    slot = b % 2

    def gather_rows(idx_ref, buf_slot):
        def body(r, carry):
            pltpu.make_async_copy(h_hbm.at[idx_ref[0, 0, r]], xbuf.at[buf_slot, r],
                                  gsem.at[buf_slot]).start()
            return carry
        lax.fori_loop(0, bm, body, 0)

    def wait_gather(buf_slot):
        pltpu.make_async_copy(h_hbm.at[pl.ds(0, bm)], xbuf.at[buf_slot], gsem.at[buf_slot]).wait()

    def wait_scatter():
        pltpu.make_async_copy(ybuf, y_hbm.at[pl.ds(0, bm)], ssem.at[0]).wait()

    @pl.when(b < n_used)
    def _():
        @pl.when(f == 0)
        def _():
            @pl.when(b == 0)
            def _():
                gather_rows(src_cur, 0)

            wait_gather(slot)

            @pl.when(b + 1 < n_used)
            def _():
                gather_rows(src_nxt, 1 - slot)

            for s in range(ROW_TILES):
                xs_ref[:, s * LANES:(s + 1) * LANES] = xbuf[slot, :, s, :].astype(BF16)
            acc_ref[...] = jnp.zeros_like(acc_ref)

        xb = xs_ref[...]
        act = (_silu(_dot(xb, w1_ref[...])) * _dot(xb, w3_ref[...])).astype(BF16)
        acc_ref[...] += _dot(act, w2_ref[...])

        @pl.when(f == pl.num_programs(1) - 1)
        def _():
            @pl.when(b > 0)
            def _():
                wait_scatter()

            for s in range(ROW_TILES):
                ybuf[:, s, :] = acc_ref[:, s * LANES:(s + 1) * LANES]

            def body(r, carry):
                pltpu.make_async_copy(ybuf.at[r], y_hbm.at[dst_cur[0, 0, r]], ssem.at[0]).start()
                return carry
            lax.fori_loop(0, bm, body, 0)

            @pl.when(b == n_used - 1)
            def _():
                wait_scatter()

    @pl.when(jnp.logical_and(b >= n_used, f == pl.num_programs(1) - 1))
    def _():
        ybuf[...] = jnp.zeros_like(ybuf)
        fill = pltpu.make_async_copy(ybuf, y_hbm.at[pl.ds(b * bm, bm)], ssem.at[0])
        fill.start()
        fill.wait()


def _moe_experts(h3, slot_src, slot_dst, block_expert, n_used, n_blocks, w1, w3, w2):
    D = D_MODEL
    bm, tf = BM_MOE, TF_MOE
    F = w1.shape[2]
    nf = F // tf
    n_slots = n_blocks * bm

    def fsel(b, f, nu):
        return jnp.where(b < nu[0], f, nf - 1)

    smem_blk = lambda imap: pl.BlockSpec((1, 1, bm), imap, memory_space=pltpu.SMEM)
    grid_spec = pltpu.PrefetchScalarGridSpec(
        num_scalar_prefetch=2,
        grid=(n_blocks, nf),
        in_specs=[
            smem_blk(lambda b, f, be, nu: (b, 0, 0)),
            smem_blk(lambda b, f, be, nu: (jnp.minimum(b + 1, n_blocks - 1), 0, 0)),
            smem_blk(lambda b, f, be, nu: (b, 0, 0)),
            pl.BlockSpec(memory_space=pl.ANY),
            pl.BlockSpec((None, D, tf), lambda b, f, be, nu: (be[b], 0, fsel(b, f, nu))),
            pl.BlockSpec((None, D, tf), lambda b, f, be, nu: (be[b], 0, fsel(b, f, nu))),
            pl.BlockSpec((None, tf, D), lambda b, f, be, nu: (be[b], fsel(b, f, nu), 0)),
        ],
        out_specs=pl.BlockSpec(memory_space=pl.ANY),
        scratch_shapes=[
            pltpu.VMEM((2, bm, ROW_TILES, LANES), F32),
            pltpu.VMEM((bm, D), BF16),
            pltpu.VMEM((bm, D), F32),
            pltpu.VMEM((bm, ROW_TILES, LANES), F32),
            pltpu.SemaphoreType.DMA((2,)),
            pltpu.SemaphoreType.DMA((1,)),
        ],
    )
    return pl.pallas_call(
        _moe_kernel,
        grid_spec=grid_spec,
        out_shape=jax.ShapeDtypeStruct((n_slots, ROW_TILES, LANES), F32),
        compiler_params=_params("arbitrary", "arbitrary"),
        name="moe_experts",
    )(block_expert, n_used, slot_src, slot_src, slot_dst, h3, w1, w3, w2)


def _combine_kernel(x_ref, mod_ref, y_ref, gate_ref, fg_ref, o_ref, f_ref):
    g = gate_ref[...]
    for s in range(ROW_TILES):
        f_ref[:, s * LANES:(s + 1) * LANES] = (g[:, 0:1] * y_ref[:, s, :]
                                               + g[:, 1:2] * y_ref[:, ROW_TILES + s, :])
    x = x_ref[...] + mod_ref[0, 5:6, :] * f_ref[...]
    o_ref[...] = (x * lax.rsqrt(jnp.mean(x * x, axis=-1, keepdims=True) + EPS)) * fg_ref[...]


def _combine(x2, mod, y_slots, gates, final_g, seq):
    T, D = x2.shape
    tm = TM_OUT
    tiles_per_seq = seq // tm
    y_pairs = y_slots.reshape(-1, TOP_K * ROW_TILES, LANES)
    row = pl.BlockSpec((tm, D), lambda i: (i, 0))
    return pl.pallas_call(
        _combine_kernel,
        grid=(T // tm,),
        in_specs=[
            row,
            pl.BlockSpec((1, N_MOD, D), lambda i: (i // tiles_per_seq, 0, 0)),
            pl.BlockSpec((tm, TOP_K * ROW_TILES, LANES), lambda i: (i, 0, 0)),
            pl.BlockSpec((tm, TOP_K), lambda i: (i, 0)),
            _resident((1, D)),
        ],
        out_specs=row,
        out_shape=jax.ShapeDtypeStruct((T, D), F32),
        scratch_shapes=[pltpu.VMEM((tm, D), F32)],
        compiler_params=_params("arbitrary"),
        name="combine_norm",
    )(x2, mod, y_pairs, gates, final_g.reshape(1, D))


def _mixer_layer(x2, mod, i, batch, seq, p):
    parts = _mixer_in(x2, mod, p["norm1_g"][i], p["w_in"][i].astype(BF16), p["b_in"][i],
                      p["sgu_ln_g"][i], p["sgu_ln_b"][i], seq)
    return _mixer_mid(x2, mod, parts, p["conv_w"][i], p["conv_b"][i], p["conv_ln_g"][i],
                      p["conv_ln_b"][i], p["conv_pw_w"][i].astype(BF16), p["conv_pw_b"][i],
                      p["pool_w"][i].astype(BF16), p["pool_b"][i], p["pool_scale"][i],
                      p["sgu_w"][i], p["sgu_b"][i], p["w_out"][i].astype(BF16), batch, seq)


def kernel(x, c, ada_w, ada_b, norm1_g, norm2_g, w_in, b_in, conv_w, conv_b, conv_ln_g, conv_ln_b, conv_pw_w, conv_pw_b, pool_w, pool_b, pool_scale, sgu_ln_g, sgu_ln_b, sgu_w, sgu_b, w_out, ffn_w1, ffn_w3, ffn_w2, router_w, moe_w1, moe_w3, moe_w2, final_norm_g):
    p = dict(norm1_g=norm1_g, w_in=w_in, b_in=b_in, conv_w=conv_w, conv_b=conv_b,
             conv_ln_g=conv_ln_g, conv_ln_b=conv_ln_b, conv_pw_w=conv_pw_w, conv_pw_b=conv_pw_b,
             pool_w=pool_w, pool_b=pool_b, pool_scale=pool_scale, sgu_ln_g=sgu_ln_g,
             sgu_ln_b=sgu_ln_b, sgu_w=sgu_w, sgu_b=sgu_b, w_out=w_out)
    batch, seq, D = x.shape
    T = batch * seq
    mod = _ada_mod(c, ada_w, ada_b)
    x2 = x.reshape(T, D)

    x2 = _mixer_layer(x2, mod[0], 0, batch, seq, p)
    x2 = _ffn(x2, mod[0], norm2_g[0], ffn_w1[0].astype(BF16), ffn_w3[0].astype(BF16),
              ffn_w2[0].astype(BF16), seq)

    x2 = _mixer_layer(x2, mod[1], 1, batch, seq, p)
    h3, top_idx, gates = _router(x2, mod[1], norm2_g[1], router_w[0], seq)
    slot_src, slot_dst, block_expert, n_used, n_blocks = _dispatch_plan(top_idx, T)
    y_slots = _moe_experts(h3, slot_src, slot_dst, block_expert, n_used, n_blocks,
                           moe_w1[0].astype(BF16), moe_w3[0].astype(BF16), moe_w2[0].astype(BF16))
    out = _combine(x2, mod[1], y_slots, gates, final_norm_g, seq)
    return out.reshape(batch, seq, D)
```

```python
import functools
import math

import jax
import jax.numpy as jnp
from jax import lax
from jax.experimental import pallas as pl
from jax.experimental.pallas import tpu as pltpu

F32 = jnp.float32
BF16 = jnp.bfloat16

D_MODEL = 1024
CONV_WIDTH = 31
POOL_WINDOWS = (2, 4, 8, 16)
POOL_GROUP = D_MODEL // len(POOL_WINDOWS)
SGU_CHUNK = 128
N_SGU_HEADS = 8
N_EXPERTS = 8
TOP_K = 2
N_MOD = 6
EPS = 1e-6

LANES = 128
SUBLANES = 8
ROW_TILES = D_MODEL // LANES

CONV_HALO = 32
POOL_HALO = 16
CONV_ROWS = 128

VMEM_LIMIT = 56 * 1024 * 1024

TM_IN = 512
TS_MIX = 512
TM_FFN = 512
TF_FFN = 1408
TM_ROUTE = 512
BM_MOE = 1024
TF_MOE = 896
TM_OUT = 512
DMA_UNROLL = 8


def _sigmoid(x):
    return 0.5 * (1.0 + jnp.tanh(0.5 * x))


def _silu(x):
    return x * _sigmoid(x)


def _gelu_tanh(x):
    c = math.sqrt(2.0 / math.pi)
    return x * (0.5 * (1.0 + jnp.tanh(c * (x + 0.044715 * (x * x * x)))))


def _dot(a, b):
    return jnp.dot(a, b, preferred_element_type=F32)


def _layernorm(x, g, b):
    mu = jnp.mean(x, axis=-1, keepdims=True)
    xc = x - mu
    var = jnp.mean(xc * xc, axis=-1, keepdims=True)
    return xc * lax.rsqrt(var + EPS) * g + b


def _ada_rmsnorm(x, g, scale, shift):
    y = x * lax.rsqrt(jnp.mean(x * x, axis=-1, keepdims=True) + EPS)
    return (y * g) * (1.0 + scale) + shift


def _params(*sem):
    return pltpu.CompilerParams(dimension_semantics=sem, vmem_limit_bytes=VMEM_LIMIT)


def _resident(shape):
    nd = len(shape)
    return pl.BlockSpec(shape, lambda *_: (0,) * nd, pipeline_mode=pl.Buffered(1))


def _ada_kernel(c_ref, w_ref, b_ref, o_ref):
    ca = _silu(c_ref[...]).astype(BF16)
    o_ref[...] = _dot(ca, w_ref[...].astype(BF16)) + b_ref[...]


def _ada_mod(c, ada_w, ada_b):
    L, D, N = ada_w.shape
    B = c.shape[0]
    out = pl.pallas_call(
        _ada_kernel,
        grid=(L, N // D),
        in_specs=[
            pl.BlockSpec((B, D), lambda l, j: (0, 0)),
            pl.BlockSpec((None, D, D), lambda l, j: (l, 0, j)),
            pl.BlockSpec((None, 1, D), lambda l, j: (l, 0, j)),
        ],
        out_specs=pl.BlockSpec((None, B, D), lambda l, j: (l, 0, j)),
        out_shape=jax.ShapeDtypeStruct((L, B, N), F32),
        compiler_params=_params("arbitrary", "arbitrary"),
        name="ada_mod",
    )(c, ada_w, ada_b.reshape(L, 1, N))
    return out.reshape(L, B, N_MOD, D)


def _mixer_in_kernel(x_ref, mod_ref, ng_ref, w_ref, b_ref, lng_ref, lnb_ref,
                     ua_ref, p_ref, su_ref, sv_ref, g0_ref, g1_ref, g2_ref):
    D = D_MODEL
    h = _ada_rmsnorm(x_ref[...], ng_ref[...], mod_ref[0, 1:2, :], mod_ref[0, 0:1, :])
    hb = h.astype(BF16)

    def proj(j):
        return _dot(hb, w_ref[:, j * D:(j + 1) * D]) + b_ref[:, j * D:(j + 1) * D]

    ua_ref[...] = (proj(0) * _sigmoid(proj(1))).astype(BF16)
    p_ref[...] = proj(2).astype(BF16)
    su_ref[...] = _gelu_tanh(proj(3)).astype(BF16)
    sv_ref[...] = _layernorm(_gelu_tanh(proj(4)), lng_ref[...], lnb_ref[...]).astype(BF16)
    g0_ref[...] = _sigmoid(proj(5)).astype(BF16)
    g1_ref[...] = _sigmoid(proj(6)).astype(BF16)
    g2_ref[...] = _sigmoid(proj(7)).astype(BF16)


def _mixer_in(x2, mod, norm_g, w_in, b_in, sgu_ln_g, sgu_ln_b, seq):
    T, D = x2.shape
    tm = TM_IN
    tiles_per_seq = seq // tm
    row = pl.BlockSpec((tm, D), lambda i: (i, 0))
    outs = pl.pallas_call(
        _mixer_in_kernel,
        grid=(T // tm,),
        in_specs=[
            row,
            pl.BlockSpec((1, N_MOD, D), lambda i: (i // tiles_per_seq, 0, 0)),
            _resident((1, D)),
            _resident(w_in.shape),
            _resident((1, w_in.shape[1])),
            _resident((1, D)),
            _resident((1, D)),
        ],
        out_specs=[row] * 7,
        out_shape=[jax.ShapeDtypeStruct((T, D), BF16)] * 7,
        compiler_params=_params("arbitrary"),
        name="mixer_in",
    )(x2, mod, norm_g.reshape(1, D), w_in, b_in.reshape(1, -1),
      sgu_ln_g.reshape(1, D), sgu_ln_b.reshape(1, D))
    return outs


def _mixer_mid_kernel(x_ref, mod_ref, ua_ref, p_ref, su_ref, sv_ref, g0_ref, g1_ref, g2_ref,
                      cw_ref, cb_ref, clg_ref, clb_ref, pw_ref, pb_ref,
                      plw_ref, plb_ref, pls_ref, sw_ref, sbt_ref, wo_ref,
                      o_ref, u_ext, p_ext, cv_ref, mg_ref):
    D = D_MODEL
    ts = TS_MIX
    s_idx = pl.program_id(1)

    @pl.when(s_idx == 0)
    def _():
        u_ext[0:CONV_HALO, :] = jnp.zeros((CONV_HALO, D), F32)
        p_ext[0:POOL_HALO, :] = jnp.zeros((POOL_HALO, D), F32)

    @pl.when(s_idx > 0)
    def _():
        u_ext[0:CONV_HALO, :] = u_ext[ts:ts + CONV_HALO, :]
        p_ext[0:POOL_HALO, :] = p_ext[ts:ts + POOL_HALO, :]

    u_ext[CONV_HALO:CONV_HALO + ts, :] = ua_ref[...].astype(F32)
    p_ext[POOL_HALO:POOL_HALO + ts, :] = p_ref[...].astype(F32)

    ri = lax.broadcasted_iota(jnp.int32, (SGU_CHUNK, SGU_CHUNK), 0)
    ci = lax.broadcasted_iota(jnp.int32, (SGU_CHUNK, SGU_CHUNK), 1)
    causal = ri >= ci
    for hd in range(N_SGU_HEADS):
        cols = slice(hd * SGU_CHUNK, (hd + 1) * SGU_CHUNK)
        wm = jnp.where(causal, sw_ref[hd], 0.0).astype(BF16)
        bcol = sbt_ref[:, hd:hd + 1]
        for n in range(ts // SGU_CHUNK):
            rows = slice(n * SGU_CHUNK, (n + 1) * SGU_CHUNK)
            mixed = _dot(wm, sv_ref[rows, cols]) + bcol
            mg_ref[rows, cols] = (g2_ref[rows, cols].astype(F32)
                                  * su_ref[rows, cols].astype(F32) * mixed)

    pos1 = (s_idx * ts + lax.broadcasted_iota(jnp.int32, (ts, 1), 0) + 1).astype(F32)
    for gi, w in enumerate(POOL_WINDOWS):
        cols = slice(gi * POOL_GROUP, (gi + 1) * POOL_GROUP)
        cur = p_ext[POOL_HALO:POOL_HALO + ts, cols]
        win = cur
        for j in range(1, w):
            win = win + p_ext[POOL_HALO - j:POOL_HALO - j + ts, cols]
        d = win * (1.0 / jnp.minimum(pos1, float(w))) - cur
        yb = (_dot(d.astype(BF16), plw_ref[gi]) + plb_ref[:, cols]) * pls_ref[:, cols]
        mg_ref[:, cols] += g1_ref[:, cols].astype(F32) * yb

    rc = CONV_ROWS
    n_a = -(-CONV_WIDTH // SUBLANES)

    def conv_rows(ci, carry):
        r0 = pl.multiple_of(ci * rc, rc)
        for ct in range(D // LANES):
            cols = slice(ct * LANES, (ct + 1) * LANES)
            acc = jnp.zeros((rc, LANES), F32) + cb_ref[:, cols]
            for r in range(SUBLANES):
                v = None
                for a in range(n_a):
                    j = SUBLANES * a + r
                    if j >= CONV_WIDTH:
                        continue
                    start = r0 + (CONV_HALO - SUBLANES - SUBLANES * a)
                    term = (u_ext[pl.ds(start, rc + SUBLANES), cols]
                            * cw_ref[CONV_WIDTH - 1 - j:CONV_WIDTH - j, cols])
                    v = term if v is None else v + term
                acc = acc + v[SUBLANES - r:SUBLANES - r + rc, :]
            cv_ref[pl.ds(r0, rc), cols] = acc
        return carry

    lax.fori_loop(0, ts // rc, conv_rows, 0)
    cn =_silu(_layernorm(cv_ref[...], clg_ref[...], clb_ref[...])).astype(BF16)
    ya = _dot(cn, pw_ref[...]) + pb_ref[...]
    merged = mg_ref[...] + g0_ref[...].astype(F32) * ya

    y = _dot(merged.astype(BF16), wo_ref[...])
    o_ref[...] = x_ref[...] + mod_ref[0, 2:3, :] * y


def _mixer_mid(x2, mod, parts, conv_w, conv_b, conv_ln_g, conv_ln_b, conv_pw_w, conv_pw_b,
               pool_w, pool_b, pool_scale, sgu_w, sgu_b, w_out, batch, seq):
    T, D = x2.shape
    ts = TS_MIX
    ns = seq // ts
    row = pl.BlockSpec((ts, D), lambda b, s: (b * ns + s, 0))
    vec = lambda a: a.reshape(1, D)
    return pl.pallas_call(
        _mixer_mid_kernel,
        grid=(batch, ns),
        in_specs=[
            row,
            pl.BlockSpec((1, N_MOD, D), lambda b, s: (b, 0, 0)),
        ] + [row] * 7 + [
            _resident((CONV_WIDTH, D)), _resident((1, D)), _resident((1, D)), _resident((1, D)),
            _resident((D, D)), _resident((1, D)),
            _resident(pool_w.shape), _resident((1, D)), _resident((1, D)),
            _resident(sgu_w.shape), _resident((SGU_CHUNK, N_SGU_HEADS)),
            _resident((D, D)),
        ],
        out_specs=row,
        out_shape=jax.ShapeDtypeStruct((T, D), F32),
        scratch_shapes=[
            pltpu.VMEM((ts + CONV_HALO, D), F32),
            pltpu.VMEM((ts + POOL_HALO, D), F32),
            pltpu.VMEM((ts, D), F32),
            pltpu.VMEM((ts, D), F32),
        ],
        compiler_params=_params("arbitrary", "arbitrary"),
        name="mixer_mid",
    )(x2, mod, *parts, conv_w, vec(conv_b), vec(conv_ln_g), vec(conv_ln_b), conv_pw_w,
      vec(conv_pw_b), pool_w, vec(pool_b), vec(pool_scale), sgu_w, sgu_b.T, w_out)


def _ffn_kernel(x_ref, mod_ref, ng_ref, w1_ref, w3_ref, w2_ref, o_ref, h_ref, acc_ref):
    f = pl.program_id(1)

    @pl.when(f == 0)
    def _():
        h = _ada_rmsnorm(x_ref[...], ng_ref[...], mod_ref[0, 4:5, :], mod_ref[0, 3:4, :])
        h_ref[...] = h.astype(BF16)
        acc_ref[...] = jnp.zeros_like(acc_ref)

    hb = h_ref[...]
    act = (_silu(_dot(hb, w1_ref[...])) * _dot(hb, w3_ref[...])).astype(BF16)
    acc_ref[...] += _dot(act, w2_ref[...])

    @pl.when(f == pl.num_programs(1) - 1)
    def _():
        o_ref[...] = x_ref[...] + mod_ref[0, 5:6, :] * acc_ref[...]


def _ffn(x2, mod, norm_g, w1, w3, w2, seq):
    T, D = x2.shape
    F = w1.shape[1]
    tm, tf = TM_FFN, TF_FFN
    tiles_per_seq = seq // tm
    row = pl.BlockSpec((tm, D), lambda i, f: (i, 0))
    return pl.pallas_call(
        _ffn_kernel,
        grid=(T // tm, F // tf),
        in_specs=[
            row,
            pl.BlockSpec((1, N_MOD, D), lambda i, f: (i // tiles_per_seq, 0, 0)),
            _resident((1, D)),
            pl.BlockSpec((D, tf), lambda i, f: (0, f)),
            pl.BlockSpec((D, tf), lambda i, f: (0, f)),
            pl.BlockSpec((tf, D), lambda i, f: (f, 0)),
        ],
        out_specs=row,
        out_shape=jax.ShapeDtypeStruct((T, D), F32),
        scratch_shapes=[pltpu.VMEM((tm, D), BF16), pltpu.VMEM((tm, D), F32)],
        compiler_params=_params("arbitrary", "arbitrary"),
        name="ffn",
    )(x2, mod, norm_g.reshape(1, D), w1, w3, w2)


def _split_bf16(v):
    hi = v.astype(BF16)
    lo = (v - hi.astype(F32)).astype(BF16)
    return hi, lo


def _router_kernel(x_ref, mod_ref, ng_ref, rw_ref, h_ref, idx_ref, gate_ref):
    tm = x_ref.shape[0]
    h = _ada_rmsnorm(x_ref[...], ng_ref[...], mod_ref[0, 4:5, :], mod_ref[0, 3:4, :])
    for s in range(ROW_TILES):
        h_ref[pl.ds(s, tm, stride=ROW_TILES), :] = h[:, s * LANES:(s + 1) * LANES]

    h_hi, h_lo = _split_bf16(h)
    w_hi, w_lo = _split_bf16(rw_ref[...])
    logits = _dot(h_hi, w_hi) + (_dot(h_lo, w_hi) + _dot(h_hi, w_lo))

    lane = lax.broadcasted_iota(jnp.int32, (tm, LANES), 1)
    neg = jnp.float32(-jnp.inf)
    logits = jnp.where(lane < N_EXPERTS, logits, neg)
    m1 = jnp.max(logits, axis=-1, keepdims=True)
    i1 = jnp.min(jnp.where(logits == m1, lane, LANES), axis=-1, keepdims=True)
    rest = jnp.where(lane == i1, neg, logits)
    m2 = jnp.max(rest, axis=-1, keepdims=True)
    i2 = jnp.min(jnp.where(rest == m2, lane, LANES), axis=-1, keepdims=True)
    e2 = jnp.exp(m2 - m1)
    den = 1.0 + e2
    idx_ref[...] = jnp.where(lane == 0, i1, i2)[:, :TOP_K]
    gate_ref[...] = jnp.where(lane == 0, 1.0 / den, e2 / den)[:, :TOP_K]


def _router(x2, mod, norm_g, router_w, seq):
    T, D = x2.shape
    tm = TM_ROUTE
    tiles_per_seq = seq // tm
    rw = jnp.zeros((D, LANES), F32).at[:, :N_EXPERTS].set(router_w)
    return pl.pallas_call(
        _router_kernel,
        grid=(T // tm,),
        in_specs=[
            pl.BlockSpec((tm, D), lambda i: (i, 0)),
            pl.BlockSpec((1, N_MOD, D), lambda i: (i // tiles_per_seq, 0, 0)),
            _resident((1, D)),
            _resident((D, LANES)),
        ],
        out_specs=[
            pl.BlockSpec((tm * ROW_TILES, LANES), lambda i: (i, 0)),
            pl.BlockSpec((tm, TOP_K), lambda i: (i, 0)),
            pl.BlockSpec((tm, TOP_K), lambda i: (i, 0)),
        ],
        out_shape=[
            jax.ShapeDtypeStruct((T * ROW_TILES, LANES), F32),
            jax.ShapeDtypeStruct((T, TOP_K), jnp.int32),
            jax.ShapeDtypeStruct((T, TOP_K), F32),
        ],
        compiler_params=_params("arbitrary"),
        name="router",
    )(x2, mod, norm_g.reshape(1, D), rw)


def _dispatch_plan(top_idx, n_tokens):
    bm = BM_MOE
    n_flat = n_tokens * TOP_K
    n_blocks = -(-(n_flat + N_EXPERTS * (bm - 1)) // bm)
    n_slots = n_blocks * bm
    flat_e = top_idx.reshape(-1)
    order = jnp.argsort(flat_e, stable=True).astype(jnp.int32)
    counts = jnp.sum(flat_e[:, None] == jnp.arange(N_EXPERTS, dtype=jnp.int32)[None, :],
                     axis=0, dtype=jnp.int32)
    padded = (counts + bm - 1) // bm * bm
    start_sorted = jnp.cumsum(counts) - counts
    ends_padded = jnp.cumsum(padded)
    start_padded = ends_padded - padded
    n_used = (ends_padded[-1] // bm).astype(jnp.int32)
    block_start = jnp.arange(n_blocks, dtype=jnp.int32) * bm
    block_expert = jnp.minimum(jnp.searchsorted(ends_padded, block_start, side="right"),
                               N_EXPERTS - 1).astype(jnp.int32)
    last_e = block_expert[jnp.maximum(n_used - 1, 0)]
    block_expert = jnp.where(jnp.arange(n_blocks) < n_used, block_expert, last_e)
    slot = jnp.arange(n_slots, dtype=jnp.int32)
    slot_e = jnp.repeat(block_expert, bm)
    rank = slot - start_padded[slot_e]
    valid = (rank < counts[slot_e]) & (slot < ends_padded[-1])
    flat = order[jnp.clip(start_sorted[slot_e] + rank, 0, n_flat - 1)]
    spare = n_flat + jnp.cumsum(jnp.logical_not(valid).astype(jnp.int32)) - 1
    slot_src = jnp.where(valid, flat // TOP_K, 0).astype(jnp.int32)
    slot_dst = jnp.where(valid, (flat % TOP_K) * n_tokens + flat // TOP_K, spare).astype(jnp.int32)
    return (slot_src.reshape(n_blocks, 1, bm), slot_dst.reshape(n_blocks, 1, bm),
            block_expert, n_used.reshape(1), n_blocks)


def _moe_kernel(be_ref, nu_ref, src_cur, src_nxt, dst_cur, h_hbm, w1_ref, w3_ref, w2_ref,
                y_hbm, xbuf, xs_ref, acc_ref, ybuf, gsem, ssem):
    bm = BM_MOE
    b = pl.program_id(0)
    f = pl.program_id(1)
    n_used = nu_ref[0]
    slot = b % 2

    rt = ROW_TILES
    blk_rows = bm * rt

    def token_rows(ref, tok):
        return ref.at[pl.ds(pl.multiple_of(tok * rt, rt), rt), :]

    def gather_rows(idx_ref, buf_slot):
        def body(r, carry):
            pltpu.make_async_copy(token_rows(h_hbm, idx_ref[0, 0, r]),
                                  token_rows(xbuf, buf_slot * bm + r), gsem.at[buf_slot]).start()
            return carry
        lax.fori_loop(0, bm, body, 0, unroll=DMA_UNROLL)

    def wait_gather(buf_slot):
        pltpu.make_async_copy(h_hbm.at[pl.ds(0, blk_rows), :],
                              xbuf.at[pl.ds(pl.multiple_of(buf_slot * blk_rows, blk_rows), blk_rows), :],
                              gsem.at[buf_slot]).wait()

    def wait_scatter():
        pltpu.make_async_copy(ybuf, y_hbm.at[pl.ds(0, blk_rows), :], ssem.at[0]).wait()

    @pl.when(b < n_used)
    def _():
        @pl.when(f == 0)
        def _():
            @pl.when(b == 0)
            def _():
                gather_rows(src_cur, 0)

            wait_gather(slot)

            @pl.when(b + 1 < n_used)
            def _():
                gather_rows(src_nxt, 1 - slot)

            base = slot * blk_rows
            for s in range(rt):
                xs_ref[:, s * LANES:(s + 1) * LANES] = (
                    xbuf[pl.ds(base + s, bm, stride=rt), :].astype(BF16))
            acc_ref[...] = jnp.zeros_like(acc_ref)

        xb = xs_ref[...]
        act = (_silu(_dot(xb, w1_ref[...])) * _dot(xb, w3_ref[...])).astype(BF16)
        acc_ref[...] += _dot(act, w2_ref[...])

        @pl.when(f == pl.num_programs(1) - 1)
        def _():
            @pl.when(b > 0)
            def _():
                wait_scatter()

            for s in range(rt):
                ybuf[pl.ds(s, bm, stride=rt), :] = acc_ref[:, s * LANES:(s + 1) * LANES]

            def body(r, carry):
                pltpu.make_async_copy(token_rows(ybuf, r), token_rows(y_hbm, dst_cur[0, 0, r]),
                                      ssem.at[0]).start()
                return carry
            lax.fori_loop(0, bm, body, 0, unroll=DMA_UNROLL)

            @pl.when(b == n_used - 1)
            def _():
                wait_scatter()

    @pl.when(jnp.logical_and(b >= n_used, f == pl.num_programs(1) - 1))
    def _():
        ybuf[...] = jnp.zeros_like(ybuf)
        fill = pltpu.make_async_copy(
            ybuf, y_hbm.at[pl.ds(pl.multiple_of(b * blk_rows, blk_rows), blk_rows), :], ssem.at[0])
        fill.start()
        fill.wait()


def _moe_experts(h3, slot_src, slot_dst, block_expert, n_used, n_blocks, w1, w3, w2):
    D = D_MODEL
    bm, tf = BM_MOE, TF_MOE
    F = w1.shape[2]
    nf = F // tf
    n_slots = n_blocks * bm

    def fsel(b, f, nu):
        return jnp.where(b < nu[0], f, nf - 1)

    smem_blk = lambda imap: pl.BlockSpec((1, 1, bm), imap, memory_space=pltpu.SMEM)
    grid_spec = pltpu.PrefetchScalarGridSpec(
        num_scalar_prefetch=2,
        grid=(n_blocks, nf),
        in_specs=[
            smem_blk(lambda b, f, be, nu: (b, 0, 0)),
            smem_blk(lambda b, f, be, nu: (jnp.minimum(b + 1, n_blocks - 1), 0, 0)),
            smem_blk(lambda b, f, be, nu: (b, 0, 0)),
            pl.BlockSpec(memory_space=pl.ANY),
            pl.BlockSpec((None, D, tf), lambda b, f, be, nu: (be[b], 0, fsel(b, f, nu))),
            pl.BlockSpec((None, D, tf), lambda b, f, be, nu: (be[b], 0, fsel(b, f, nu))),
            pl.BlockSpec((None, tf, D), lambda b, f, be, nu: (be[b], fsel(b, f, nu), 0)),
        ],
        out_specs=pl.BlockSpec(memory_space=pl.ANY),
        scratch_shapes=[
            pltpu.VMEM((2 * bm * ROW_TILES, LANES), F32),
            pltpu.VMEM((bm, D), BF16),
            pltpu.VMEM((bm, D), F32),
            pltpu.VMEM((bm * ROW_TILES, LANES), F32),
            pltpu.SemaphoreType.DMA((2,)),
            pltpu.SemaphoreType.DMA((1,)),
        ],
    )
    return pl.pallas_call(
        _moe_kernel,
        grid_spec=grid_spec,
        out_shape=jax.ShapeDtypeStruct((n_slots * ROW_TILES, LANES), F32),
        compiler_params=_params("arbitrary", "arbitrary"),
        name="moe_experts",
    )(block_expert, n_used, slot_src, slot_src, slot_dst, h3, w1, w3, w2)


def _combine_kernel(x_ref, mod_ref, y0_ref, y1_ref, gate_ref, fg_ref, o_ref, f_ref):
    tm = x_ref.shape[0]
    g = gate_ref[...]
    for s in range(ROW_TILES):
        rows = pl.ds(s, tm, stride=ROW_TILES)
        f_ref[:, s * LANES:(s + 1) * LANES] = g[:, 0:1] * y0_ref[rows, :] + g[:, 1:2] * y1_ref[rows, :]
    x = x_ref[...] + mod_ref[0, 5:6, :] * f_ref[...]
    o_ref[...] = (x * lax.rsqrt(jnp.mean(x * x, axis=-1, keepdims=True) + EPS)) * fg_ref[...]


def _combine(x2, mod, y_slots, gates, final_g, seq):
    T, D = x2.shape
    tm = TM_OUT
    tiles_per_seq = seq // tm
    n_tiles = T // tm
    row = pl.BlockSpec((tm, D), lambda i: (i, 0))
    return pl.pallas_call(
        _combine_kernel,
        grid=(T // tm,),
        in_specs=[
            row,
            pl.BlockSpec((1, N_MOD, D), lambda i: (i // tiles_per_seq, 0, 0)),
            pl.BlockSpec((tm * ROW_TILES, LANES), lambda i: (i, 0)),
            pl.BlockSpec((tm * ROW_TILES, LANES), lambda i: (n_tiles + i, 0)),
            pl.BlockSpec((tm, TOP_K), lambda i: (i, 0)),
            _resident((1, D)),
        ],
        out_specs=row,
        out_shape=jax.ShapeDtypeStruct((T, D), F32),
        scratch_shapes=[pltpu.VMEM((tm, D), F32)],
        compiler_params=_params("arbitrary"),
        name="combine_norm",
    )(x2, mod, y_slots, y_slots, gates, final_g.reshape(1, D))


def _mixer_layer(x2, mod, i, batch, seq, p):
    parts = _mixer_in(x2, mod, p["norm1_g"][i], p["w_in"][i].astype(BF16), p["b_in"][i],
                      p["sgu_ln_g"][i], p["sgu_ln_b"][i], seq)
    return _mixer_mid(x2, mod, parts, p["conv_w"][i], p["conv_b"][i], p["conv_ln_g"][i],
                      p["conv_ln_b"][i], p["conv_pw_w"][i].astype(BF16), p["conv_pw_b"][i],
                      p["pool_w"][i].astype(BF16), p["pool_b"][i], p["pool_scale"][i],
                      p["sgu_w"][i], p["sgu_b"][i], p["w_out"][i].astype(BF16), batch, seq)


def kernel(x, c, ada_w, ada_b, norm1_g, norm2_g, w_in, b_in, conv_w, conv_b, conv_ln_g, conv_ln_b, conv_pw_w, conv_pw_b, pool_w, pool_b, pool_scale, sgu_ln_g, sgu_ln_b, sgu_w, sgu_b, w_out, ffn_w1, ffn_w3, ffn_w2, router_w, moe_w1, moe_w3, moe_w2, final_norm_g):
    p = dict(norm1_g=norm1_g, w_in=w_in, b_in=b_in, conv_w=conv_w, conv_b=conv_b,
             conv_ln_g=conv_ln_g, conv_ln_b=conv_ln_b, conv_pw_w=conv_pw_w, conv_pw_b=conv_pw_b,
             pool_w=pool_w, pool_b=pool_b, pool_scale=pool_scale, sgu_ln_g=sgu_ln_g,
             sgu_ln_b=sgu_ln_b, sgu_w=sgu_w, sgu_b=sgu_b, w_out=w_out)
    batch, seq, D = x.shape
    T = batch * seq
    mod = _ada_mod(c, ada_w, ada_b)
    x2 = x.reshape(T, D)

    x2 = _mixer_layer(x2, mod[0], 0, batch, seq, p)
    x2 = _ffn(x2, mod[0], norm2_g[0], ffn_w1[0].astype(BF16), ffn_w3[0].astype(BF16),
              ffn_w2[0].astype(BF16), seq)

    x2 = _mixer_layer(x2, mod[1], 1, batch, seq, p)
    h3, top_idx, gates = _router(x2, mod[1], norm2_g[1], router_w[0], seq)
    slot_src, slot_dst, block_expert, n_used, n_blocks = _dispatch_plan(top_idx, T)
    y_slots = _moe_experts(h3, slot_src, slot_dst, block_expert, n_used, n_blocks,
                           moe_w1[0].astype(BF16), moe_w3[0].astype(BF16), moe_w2[0].astype(BF16))
    out = _combine(x2, mod[1], y_slots, gates, final_norm_g, seq)
    return out.reshape(batch, seq, D)
```

```python
import functools
import math

import jax
import jax.numpy as jnp
from jax import lax
from jax.experimental import pallas as pl
from jax.experimental.pallas import tpu as pltpu

F32 = jnp.float32
BF16 = jnp.bfloat16

D_MODEL = 1024
CONV_WIDTH = 31
POOL_WINDOWS = (2, 4, 8, 16)
POOL_GROUP = D_MODEL // len(POOL_WINDOWS)
SGU_CHUNK = 128
N_SGU_HEADS = 8
N_EXPERTS = 8
TOP_K = 2
N_MOD = 6
EPS = 1e-6

LANES = 128
SUBLANES = 8
ROW_TILES = D_MODEL // LANES

CONV_HALO = 32
POOL_HALO = 16
CONV_ROWS = 128

VMEM_LIMIT = 56 * 1024 * 1024

TM_IN = 512
TS_MIX = 512
TM_FFN = 512
TM_ROUTE = 512
BM_MOE = 512
TF_MOE = 1792
TM_OUT = 512
DMA_UNROLL = 8


def _sigmoid(x):
    return 0.5 * (1.0 + jnp.tanh(0.5 * x))


def _silu(x):
    return x * _sigmoid(x)


def _gelu_tanh(x):
    c = math.sqrt(2.0 / math.pi)
    return x * (0.5 * (1.0 + jnp.tanh(c * (x + 0.044715 * (x * x * x)))))


def _dot(a, b):
    return jnp.dot(a, b, preferred_element_type=F32)


def _layernorm(x, g, b):
    mu = jnp.mean(x, axis=-1, keepdims=True)
    xc = x - mu
    var = jnp.mean(xc * xc, axis=-1, keepdims=True)
    return xc * lax.rsqrt(var + EPS) * g + b


def _ada_rmsnorm(x, g, scale, shift):
    y = x * lax.rsqrt(jnp.mean(x * x, axis=-1, keepdims=True) + EPS)
    return (y * g) * (1.0 + scale) + shift


def _params(*sem):
    return pltpu.CompilerParams(dimension_semantics=sem, vmem_limit_bytes=VMEM_LIMIT)


def _resident(shape):
    nd = len(shape)
    return pl.BlockSpec(shape, lambda *_: (0,) * nd, pipeline_mode=pl.Buffered(1))


def _ada_kernel(c_ref, w_ref, b_ref, o_ref):
    ca = _silu(c_ref[...]).astype(BF16)
    o_ref[...] = _dot(ca, w_ref[...].astype(BF16)) + b_ref[...]


def _ada_mod(c, ada_w, ada_b):
    L, D, N = ada_w.shape
    B = c.shape[0]
    out = pl.pallas_call(
        _ada_kernel,
        grid=(L, N // D),
        in_specs=[
            pl.BlockSpec((B, D), lambda l, j: (0, 0)),
            pl.BlockSpec((None, D, D), lambda l, j: (l, 0, j)),
            pl.BlockSpec((None, 1, D), lambda l, j: (l, 0, j)),
        ],
        out_specs=pl.BlockSpec((None, B, D), lambda l, j: (l, 0, j)),
        out_shape=jax.ShapeDtypeStruct((L, B, N), F32),
        compiler_params=_params("arbitrary", "arbitrary"),
        name="ada_mod",
    )(c, ada_w, ada_b.reshape(L, 1, N))
    return out.reshape(L, B, N_MOD, D)


def _mixer_in_kernel(x_ref, mod_ref, ng_ref, w_ref, b_ref, lng_ref, lnb_ref,
                     ua_ref, p_ref, su_ref, sv_ref, g0_ref, g1_ref, g2_ref):
    D = D_MODEL
    h = _ada_rmsnorm(x_ref[...], ng_ref[...], mod_ref[0, 1:2, :], mod_ref[0, 0:1, :])
    hb = h.astype(BF16)

    def proj(j):
        return _dot(hb, w_ref[:, j * D:(j + 1) * D]) + b_ref[:, j * D:(j + 1) * D]

    ua_ref[...] = (proj(0) * _sigmoid(proj(1))).astype(BF16)
    p_ref[...] = proj(2).astype(BF16)
    su_ref[...] = _gelu_tanh(proj(3)).astype(BF16)
    sv_ref[...] = _layernorm(_gelu_tanh(proj(4)), lng_ref[...], lnb_ref[...]).astype(BF16)
    g0_ref[...] = _sigmoid(proj(5)).astype(BF16)
    g1_ref[...] = _sigmoid(proj(6)).astype(BF16)
    g2_ref[...] = _sigmoid(proj(7)).astype(BF16)


def _mixer_in(x2, mod, norm_g, w_in, b_in, sgu_ln_g, sgu_ln_b, seq):
    T, D = x2.shape
    tm = TM_IN
    tiles_per_seq = seq // tm
    row = pl.BlockSpec((tm, D), lambda i: (i, 0))
    outs = pl.pallas_call(
        _mixer_in_kernel,
        grid=(T // tm,),
        in_specs=[
            row,
            pl.BlockSpec((1, N_MOD, D), lambda i: (i // tiles_per_seq, 0, 0)),
            _resident((1, D)),
            _resident(w_in.shape),
            _resident((1, w_in.shape[1])),
            _resident((1, D)),
            _resident((1, D)),
        ],
        out_specs=[row] * 7,
        out_shape=[jax.ShapeDtypeStruct((T, D), BF16)] * 7,
        compiler_params=_params("arbitrary"),
        name="mixer_in",
    )(x2, mod, norm_g.reshape(1, D), w_in, b_in.reshape(1, -1),
      sgu_ln_g.reshape(1, D), sgu_ln_b.reshape(1, D))
    return outs


def _mixer_mid_kernel(x_ref, mod_ref, ua_ref, p_ref, su_ref, sv_ref, g0_ref, g1_ref, g2_ref,
                      cw_ref, cb_ref, clg_ref, clb_ref, pw_ref, pb_ref,
                      plw_ref, plb_ref, pls_ref, sw_ref, sbt_ref, wo_ref,
                      o_ref, u_ext, p_ext, cv_ref, mg_ref):
    D = D_MODEL
    ts = TS_MIX
    s_idx = pl.program_id(1)

    @pl.when(s_idx == 0)
    def _():
        u_ext[0:CONV_HALO, :] = jnp.zeros((CONV_HALO, D), F32)
        p_ext[0:POOL_HALO, :] = jnp.zeros((POOL_HALO, D), F32)

    @pl.when(s_idx > 0)
    def _():
        u_ext[0:CONV_HALO, :] = u_ext[ts:ts + CONV_HALO, :]
        p_ext[0:POOL_HALO, :] = p_ext[ts:ts + POOL_HALO, :]

    u_ext[CONV_HALO:CONV_HALO + ts, :] = ua_ref[...].astype(F32)
    p_ext[POOL_HALO:POOL_HALO + ts, :] = p_ref[...].astype(F32)

    ri = lax.broadcasted_iota(jnp.int32, (SGU_CHUNK, SGU_CHUNK), 0)
    ci = lax.broadcasted_iota(jnp.int32, (SGU_CHUNK, SGU_CHUNK), 1)
    causal = ri >= ci
    for hd in range(N_SGU_HEADS):
        cols = slice(hd * SGU_CHUNK, (hd + 1) * SGU_CHUNK)
        wm = jnp.where(causal, sw_ref[hd], 0.0).astype(BF16)
        bcol = sbt_ref[:, hd:hd + 1]
        for n in range(ts // SGU_CHUNK):
            rows = slice(n * SGU_CHUNK, (n + 1) * SGU_CHUNK)
            mixed = _dot(wm, sv_ref[rows, cols]) + bcol
            mg_ref[rows, cols] = (g2_ref[rows, cols].astype(F32)
                                  * su_ref[rows, cols].astype(F32) * mixed)

    pos1 = (s_idx * ts + lax.broadcasted_iota(jnp.int32, (ts, 1), 0) + 1).astype(F32)
    for gi, w in enumerate(POOL_WINDOWS):
        cols = slice(gi * POOL_GROUP, (gi + 1) * POOL_GROUP)
        cur = p_ext[POOL_HALO:POOL_HALO + ts, cols]
        win = cur
        for j in range(1, w):
            win = win + p_ext[POOL_HALO - j:POOL_HALO - j + ts, cols]
        d = win * (1.0 / jnp.minimum(pos1, float(w))) - cur
        yb = (_dot(d.astype(BF16), plw_ref[gi]) + plb_ref[:, cols]) * pls_ref[:, cols]
        mg_ref[:, cols] += g1_ref[:, cols].astype(F32) * yb

    rc = CONV_ROWS
    n_a = -(-CONV_WIDTH // SUBLANES)

    def conv_rows(ci, carry):
        r0 = pl.multiple_of(ci * rc, rc)
        for ct in range(D // LANES):
            cols = slice(ct * LANES, (ct + 1) * LANES)
            acc = jnp.zeros((rc, LANES), F32) + cb_ref[:, cols]
            for r in range(SUBLANES):
                v = None
                for a in range(n_a):
                    j = SUBLANES * a + r
                    if j >= CONV_WIDTH:
                        continue
                    start = r0 + (CONV_HALO - SUBLANES - SUBLANES * a)
                    term = (u_ext[pl.ds(start, rc + SUBLANES), cols]
                            * cw_ref[CONV_WIDTH - 1 - j:CONV_WIDTH - j, cols])
                    v = term if v is None else v + term
                acc = acc + v[SUBLANES - r:SUBLANES - r + rc, :]
            cv_ref[pl.ds(r0, rc), cols] = acc
        return carry

    lax.fori_loop(0, ts // rc, conv_rows, 0)
    cn = _silu(_layernorm(cv_ref[...], clg_ref[...], clb_ref[...])).astype(BF16)
    ya = _dot(cn, pw_ref[...]) + pb_ref[...]
    merged = mg_ref[...] + g0_ref[...].astype(F32) * ya

    y = _dot(merged.astype(BF16), wo_ref[...])
    o_ref[...] = x_ref[...] + mod_ref[0, 2:3, :] * y


def _mixer_mid(x2, mod, parts, conv_w, conv_b, conv_ln_g, conv_ln_b, conv_pw_w, conv_pw_b,
               pool_w, pool_b, pool_scale, sgu_w, sgu_b, w_out, batch, seq):
    T, D = x2.shape
    ts = TS_MIX
    ns = seq // ts
    row = pl.BlockSpec((ts, D), lambda b, s: (b * ns + s, 0))
    vec = lambda a: a.reshape(1, D)
    return pl.pallas_call(
        _mixer_mid_kernel,
        grid=(batch, ns),
        in_specs=[
            row,
            pl.BlockSpec((1, N_MOD, D), lambda b, s: (b, 0, 0)),
        ] + [row] * 7 + [
            _resident((CONV_WIDTH, D)), _resident((1, D)), _resident((1, D)), _resident((1, D)),
            _resident((D, D)), _resident((1, D)),
            _resident(pool_w.shape), _resident((1, D)), _resident((1, D)),
            _resident(sgu_w.shape), _resident((SGU_CHUNK, N_SGU_HEADS)),
            _resident((D, D)),
        ],
        out_specs=row,
        out_shape=jax.ShapeDtypeStruct((T, D), F32),
        scratch_shapes=[
            pltpu.VMEM((ts + CONV_HALO, D), F32),
            pltpu.VMEM((ts + POOL_HALO, D), F32),
            pltpu.VMEM((ts, D), F32),
            pltpu.VMEM((ts, D), F32),
        ],
        compiler_params=_params("arbitrary", "arbitrary"),
        name="mixer_mid",
    )(x2, mod, *parts, conv_w, vec(conv_b), vec(conv_ln_g), vec(conv_ln_b), conv_pw_w,
      vec(conv_pw_b), pool_w, vec(pool_b), vec(pool_scale), sgu_w, sgu_b.T, w_out)


def _ffn_kernel(x_ref, mod_ref, ng_ref, w1_ref, w3_ref, w2_ref, o_ref):
    x = x_ref[...]
    hb = _ada_rmsnorm(x, ng_ref[...], mod_ref[0, 4:5, :], mod_ref[0, 3:4, :]).astype(BF16)
    act = (_silu(_dot(hb, w1_ref[...])) * _dot(hb, w3_ref[...])).astype(BF16)
    o_ref[...] = x + mod_ref[0, 5:6, :] * _dot(act, w2_ref[...])


def _ffn(x2, mod, norm_g, w1, w3, w2, seq):
    T, D = x2.shape
    tm = TM_FFN
    tiles_per_seq = seq // tm
    row = pl.BlockSpec((tm, D), lambda i: (i, 0))
    return pl.pallas_call(
        _ffn_kernel,
        grid=(T // tm,),
        in_specs=[
            row,
            pl.BlockSpec((1, N_MOD, D), lambda i: (i // tiles_per_seq, 0, 0)),
            _resident((1, D)),
            _resident(w1.shape),
            _resident(w3.shape),
            _resident(w2.shape),
        ],
        out_specs=row,
        out_shape=jax.ShapeDtypeStruct((T, D), F32),
        compiler_params=_params("arbitrary"),
        name="ffn",
    )(x2, mod, norm_g.reshape(1, D), w1, w3, w2)


def _split_bf16(v):
    hi = v.astype(BF16)
    lo = (v - hi.astype(F32)).astype(BF16)
    return hi, lo


def _router_kernel(x_ref, mod_ref, ng_ref, rw_ref, h_ref, idx_ref, gate_ref):
    tm = x_ref.shape[0]
    h = _ada_rmsnorm(x_ref[...], ng_ref[...], mod_ref[0, 4:5, :], mod_ref[0, 3:4, :])
    for s in range(ROW_TILES):
        h_ref[pl.ds(s, tm, stride=ROW_TILES), :] = h[:, s * LANES:(s + 1) * LANES]

    h_hi, h_lo = _split_bf16(h)
    w_hi, w_lo = _split_bf16(rw_ref[...])
    logits = _dot(h_hi, w_hi) + (_dot(h_lo, w_hi) + _dot(h_hi, w_lo))

    lane = lax.broadcasted_iota(jnp.int32, (tm, LANES), 1)
    neg = jnp.float32(-jnp.inf)
    logits = jnp.where(lane < N_EXPERTS, logits, neg)
    m1 = jnp.max(logits, axis=-1, keepdims=True)
    i1 = jnp.min(jnp.where(logits == m1, lane, LANES), axis=-1, keepdims=True)
    rest = jnp.where(lane == i1, neg, logits)
    m2 = jnp.max(rest, axis=-1, keepdims=True)
    i2 = jnp.min(jnp.where(rest == m2, lane, LANES), axis=-1, keepdims=True)
    e2 = jnp.exp(m2 - m1)
    den = 1.0 + e2
    idx_ref[...] = jnp.where(lane == 0, i1, i2)[:, :TOP_K]
    gate_ref[...] = jnp.where(lane == 0, 1.0 / den, e2 / den)[:, :TOP_K]


def _router(x2, mod, norm_g, router_w, seq):
    T, D = x2.shape
    tm = TM_ROUTE
    tiles_per_seq = seq // tm
    rw = jnp.zeros((D, LANES), F32).at[:, :N_EXPERTS].set(router_w)
    return pl.pallas_call(
        _router_kernel,
        grid=(T // tm,),
        in_specs=[
            pl.BlockSpec((tm, D), lambda i: (i, 0)),
            pl.BlockSpec((1, N_MOD, D), lambda i: (i // tiles_per_seq, 0, 0)),
            _resident((1, D)),
            _resident((D, LANES)),
        ],
        out_specs=[
            pl.BlockSpec((tm * ROW_TILES, LANES), lambda i: (i, 0)),
            pl.BlockSpec((tm, TOP_K), lambda i: (i, 0)),
            pl.BlockSpec((tm, TOP_K), lambda i: (i, 0)),
        ],
        out_shape=[
            jax.ShapeDtypeStruct((T * ROW_TILES, LANES), F32),
            jax.ShapeDtypeStruct((T, TOP_K), jnp.int32),
            jax.ShapeDtypeStruct((T, TOP_K), F32),
        ],
        compiler_params=_params("arbitrary"),
        name="router",
    )(x2, mod, norm_g.reshape(1, D), rw)


def _dispatch_plan(top_idx, n_tokens):
    bm = BM_MOE
    n_flat = n_tokens * TOP_K
    n_blocks = -(-(n_flat + N_EXPERTS * (bm - 1)) // bm)
    n_slots = n_blocks * bm
    flat_e = top_idx.reshape(-1)
    order = jnp.argsort(flat_e, stable=True).astype(jnp.int32)
    counts = jnp.sum(flat_e[:, None] == jnp.arange(N_EXPERTS, dtype=jnp.int32)[None, :],
                     axis=0, dtype=jnp.int32)
    padded = (counts + bm - 1) // bm * bm
    start_sorted = jnp.cumsum(counts) - counts
    ends_padded = jnp.cumsum(padded)
    start_padded = ends_padded - padded
    n_used = (ends_padded[-1] // bm).astype(jnp.int32)
    block_start = jnp.arange(n_blocks, dtype=jnp.int32) * bm
    block_expert = jnp.minimum(jnp.searchsorted(ends_padded, block_start, side="right"),
                               N_EXPERTS - 1).astype(jnp.int32)
    last_e = block_expert[jnp.maximum(n_used - 1, 0)]
    block_expert = jnp.where(jnp.arange(n_blocks) < n_used, block_expert, last_e)
    slot = jnp.arange(n_slots, dtype=jnp.int32)
    slot_e = jnp.repeat(block_expert, bm)
    rank = slot - start_padded[slot_e]
    valid = (rank < counts[slot_e]) & (slot < ends_padded[-1])
    flat = order[jnp.clip(start_sorted[slot_e] + rank, 0, n_flat - 1)]
    spare = n_flat + jnp.cumsum(jnp.logical_not(valid).astype(jnp.int32)) - 1
    slot_src = jnp.where(valid, flat // TOP_K, 0).astype(jnp.int32)
    slot_dst = jnp.where(valid, (flat % TOP_K) * n_tokens + flat // TOP_K, spare).astype(jnp.int32)
    return (slot_src.reshape(n_blocks, 1, bm), slot_dst.reshape(n_blocks, 1, bm),
            block_expert, n_used.reshape(1), n_blocks)


def _moe_kernel(be_ref, nu_ref, src_cur, src_nxt, dst_cur, h_hbm, w1_ref, w3_ref, w2_ref,
                y_hbm, xbuf, xs_ref, acc_ref, ybuf, gsem, ssem):
    bm = BM_MOE
    b = pl.program_id(0)
    f = pl.program_id(1)
    n_used = nu_ref[0]
    slot = b % 2

    rt = ROW_TILES
    blk_rows = bm * rt

    def token_rows(ref, tok):
        return ref.at[pl.ds(pl.multiple_of(tok * rt, rt), rt), :]

    def gather_rows(idx_ref, buf_slot):
        def body(r, carry):
            pltpu.make_async_copy(token_rows(h_hbm, idx_ref[0, 0, r]),
                                  token_rows(xbuf, buf_slot * bm + r), gsem.at[buf_slot]).start()
            return carry
        lax.fori_loop(0, bm, body, 0, unroll=DMA_UNROLL)

    def wait_gather(buf_slot):
        pltpu.make_async_copy(h_hbm.at[pl.ds(0, blk_rows), :],
                              xbuf.at[pl.ds(pl.multiple_of(buf_slot * blk_rows, blk_rows), blk_rows), :],
                              gsem.at[buf_slot]).wait()

    def wait_scatter():
        pltpu.make_async_copy(ybuf, y_hbm.at[pl.ds(0, blk_rows), :], ssem.at[0]).wait()

    @pl.when(b < n_used)
    def _():
        @pl.when(f == 0)
        def _():
            @pl.when(b == 0)
            def _():
                gather_rows(src_cur, 0)

            wait_gather(slot)

            @pl.when(b + 1 < n_used)
            def _():
                gather_rows(src_nxt, 1 - slot)

            base = slot * blk_rows
            for s in range(rt):
                xs_ref[:, s * LANES:(s + 1) * LANES] = (
                    xbuf[pl.ds(base + s, bm, stride=rt), :].astype(BF16))
            acc_ref[...] = jnp.zeros_like(acc_ref)

        xb = xs_ref[...]
        act = (_silu(_dot(xb, w1_ref[...])) * _dot(xb, w3_ref[...])).astype(BF16)
        acc_ref[...] += _dot(act, w2_ref[...])

        @pl.when(f == pl.num_programs(1) - 1)
        def _():
            @pl.when(b > 0)
            def _():
                wait_scatter()

            for s in range(rt):
                ybuf[pl.ds(s, bm, stride=rt), :] = acc_ref[:, s * LANES:(s + 1) * LANES]

            def body(r, carry):
                pltpu.make_async_copy(token_rows(ybuf, r), token_rows(y_hbm, dst_cur[0, 0, r]),
                                      ssem.at[0]).start()
                return carry
            lax.fori_loop(0, bm, body, 0, unroll=DMA_UNROLL)

            @pl.when(b == n_used - 1)
            def _():
                wait_scatter()

    @pl.when(jnp.logical_and(b >= n_used, f == pl.num_programs(1) - 1))
    def _():
        ybuf[...] = jnp.zeros_like(ybuf)
        fill = pltpu.make_async_copy(
            ybuf, y_hbm.at[pl.ds(pl.multiple_of(b * blk_rows, blk_rows), blk_rows), :], ssem.at[0])
        fill.start()
        fill.wait()


def _moe_experts(h3, slot_src, slot_dst, block_expert, n_used, n_blocks, w1, w3, w2):
    D = D_MODEL
    bm, tf = BM_MOE, TF_MOE
    F = w1.shape[2]
    nf = F // tf
    n_slots = n_blocks * bm

    def fsel(b, f, nu):
        return jnp.where(b < nu[0], f, nf - 1)

    smem_blk = lambda imap: pl.BlockSpec((1, 1, bm), imap, memory_space=pltpu.SMEM)
    grid_spec = pltpu.PrefetchScalarGridSpec(
        num_scalar_prefetch=2,
        grid=(n_blocks, nf),
        in_specs=[
            smem_blk(lambda b, f, be, nu: (b, 0, 0)),
            smem_blk(lambda b, f, be, nu: (jnp.minimum(b + 1, n_blocks - 1), 0, 0)),
            smem_blk(lambda b, f, be, nu: (b, 0, 0)),
            pl.BlockSpec(memory_space=pl.ANY),
            pl.BlockSpec((None, D, tf), lambda b, f, be, nu: (be[b], 0, fsel(b, f, nu))),
            pl.BlockSpec((None, D, tf), lambda b, f, be, nu: (be[b], 0, fsel(b, f, nu))),
            pl.BlockSpec((None, tf, D), lambda b, f, be, nu: (be[b], fsel(b, f, nu), 0)),
        ],
        out_specs=pl.BlockSpec(memory_space=pl.ANY),
        scratch_shapes=[
            pltpu.VMEM((2 * bm * ROW_TILES, LANES), F32),
            pltpu.VMEM((bm, D), BF16),
            pltpu.VMEM((bm, D), F32),
            pltpu.VMEM((bm * ROW_TILES, LANES), F32),
            pltpu.SemaphoreType.DMA((2,)),
            pltpu.SemaphoreType.DMA((1,)),
        ],
    )
    return pl.pallas_call(
        _moe_kernel,
        grid_spec=grid_spec,
        out_shape=jax.ShapeDtypeStruct((n_slots * ROW_TILES, LANES), F32),
        compiler_params=_params("arbitrary", "arbitrary"),
        name="moe_experts",
    )(block_expert, n_used, slot_src, slot_src, slot_dst, h3, w1, w3, w2)


def _combine_kernel(x_ref, mod_ref, y0_ref, y1_ref, gate_ref, fg_ref, o_ref, f_ref):
    tm = x_ref.shape[0]
    g = gate_ref[...]
    for s in range(ROW_TILES):
        rows = pl.ds(s, tm, stride=ROW_TILES)
        f_ref[:, s * LANES:(s + 1) * LANES] = g[:, 0:1] * y0_ref[rows, :] + g[:, 1:2] * y1_ref[rows, :]
    x = x_ref[...] + mod_ref[0, 5:6, :] * f_ref[...]
    o_ref[...] = (x * lax.rsqrt(jnp.mean(x * x, axis=-1, keepdims=True) + EPS)) * fg_ref[...]


def _combine(x2, mod, y_slots, gates, final_g, seq):
    T, D = x2.shape
    tm = TM_OUT
    tiles_per_seq = seq // tm
    n_tiles = T // tm
    row = pl.BlockSpec((tm, D), lambda i: (i, 0))
    return pl.pallas_call(
        _combine_kernel,
        grid=(T // tm,),
        in_specs=[
            row,
            pl.BlockSpec((1, N_MOD, D), lambda i: (i // tiles_per_seq, 0, 0)),
            pl.BlockSpec((tm * ROW_TILES, LANES), lambda i: (i, 0)),
            pl.BlockSpec((tm * ROW_TILES, LANES), lambda i: (n_tiles + i, 0)),
            pl.BlockSpec((tm, TOP_K), lambda i: (i, 0)),
            _resident((1, D)),
        ],
        out_specs=row,
        out_shape=jax.ShapeDtypeStruct((T, D), F32),
        scratch_shapes=[pltpu.VMEM((tm, D), F32)],
        compiler_params=_params("arbitrary"),
        name="combine_norm",
    )(x2, mod, y_slots, y_slots, gates, final_g.reshape(1, D))


def _mixer_layer(x2, mod, i, batch, seq, p):
    parts = _mixer_in(x2, mod, p["norm1_g"][i], p["w_in"][i].astype(BF16), p["b_in"][i],
                      p["sgu_ln_g"][i], p["sgu_ln_b"][i], seq)
    return _mixer_mid(x2, mod, parts, p["conv_w"][i], p["conv_b"][i], p["conv_ln_g"][i],
                      p["conv_ln_b"][i], p["conv_pw_w"][i].astype(BF16), p["conv_pw_b"][i],
                      p["pool_w"][i].astype(BF16), p["pool_b"][i], p["pool_scale"][i],
                      p["sgu_w"][i], p["sgu_b"][i], p["w_out"][i].astype(BF16), batch, seq)


def kernel(x, c, ada_w, ada_b, norm1_g, norm2_g, w_in, b_in, conv_w, conv_b, conv_ln_g, conv_ln_b, conv_pw_w, conv_pw_b, pool_w, pool_b, pool_scale, sgu_ln_g, sgu_ln_b, sgu_w, sgu_b, w_out, ffn_w1, ffn_w3, ffn_w2, router_w, moe_w1, moe_w3, moe_w2, final_norm_g):
    p = dict(norm1_g=norm1_g, w_in=w_in, b_in=b_in, conv_w=conv_w, conv_b=conv_b,
             conv_ln_g=conv_ln_g, conv_ln_b=conv_ln_b, conv_pw_w=conv_pw_w, conv_pw_b=conv_pw_b,
             pool_w=pool_w, pool_b=pool_b, pool_scale=pool_scale, sgu_ln_g=sgu_ln_g,
             sgu_ln_b=sgu_ln_b, sgu_w=sgu_w, sgu_b=sgu_b, w_out=w_out)
    batch, seq, D = x.shape
    T = batch * seq
    mod = _ada_mod(c, ada_w, ada_b)
    x2 = x.reshape(T, D)

    x2 = _mixer_layer(x2, mod[0], 0, batch, seq, p)
    x2 = _ffn(x2, mod[0], norm2_g[0], ffn_w1[0].astype(BF16), ffn_w3[0].astype(BF16),
              ffn_w2[0].astype(BF16), seq)

    x2 = _mixer_layer(x2, mod[1], 1, batch, seq, p)
    h3, top_idx, gates = _router(x2, mod[1], norm2_g[1], router_w[0], seq)
    slot_src, slot_dst, block_expert, n_used, n_blocks = _dispatch_plan(top_idx, T)
    y_slots = _moe_experts(h3, slot_src, slot_dst, block_expert, n_used, n_blocks,
                           moe_w1[0].astype(BF16), moe_w3[0].astype(BF16), moe_w2[0].astype(BF16))
    out = _combine(x2, mod[1], y_slots, gates, final_norm_g, seq)
    return out.reshape(batch, seq, D)
```

```python
import functools
import math

import jax
import jax.numpy as jnp
from jax import lax
from jax.experimental import pallas as pl
from jax.experimental.pallas import tpu as pltpu

F32 = jnp.float32
BF16 = jnp.bfloat16

D_MODEL = 1024
CONV_WIDTH = 31
POOL_WINDOWS = (2, 4, 8, 16)
POOL_GROUP = D_MODEL // len(POOL_WINDOWS)
SGU_CHUNK = 128
N_SGU_HEADS = 8
N_EXPERTS = 8
TOP_K = 2
N_MOD = 6
EPS = 1e-6

LANES = 128
SUBLANES = 8
ROW_TILES = D_MODEL // LANES

CONV_HALO = 32
POOL_HALO = 32
CONV_ROWS = 128

VMEM_LIMIT = 56 * 1024 * 1024

TM_IN = 512
TS_MIX = 512
TM_FFN = 512
TM_ROUTE = 512
BM_MOE = 512
TF_MOE = 1792
TM_OUT = 512
DMA_UNROLL = 8


def _sigmoid(x):
    return 0.5 * (1.0 + jnp.tanh(0.5 * x))


def _silu(x):
    return x * _sigmoid(x)


def _gelu_tanh(x):
    c = math.sqrt(2.0 / math.pi)
    return x * (0.5 * (1.0 + jnp.tanh(c * (x + 0.044715 * (x * x * x)))))


def _dot(a, b):
    return jnp.dot(a, b, preferred_element_type=F32)


def _layernorm(x, g, b):
    mu = jnp.mean(x, axis=-1, keepdims=True)
    xc = x - mu
    var = jnp.mean(xc * xc, axis=-1, keepdims=True)
    return xc * lax.rsqrt(var + EPS) * g + b


def _ada_rmsnorm(x, g, scale, shift):
    y = x * lax.rsqrt(jnp.mean(x * x, axis=-1, keepdims=True) + EPS)
    return (y * g) * (1.0 + scale) + shift


def _params(*sem):
    return pltpu.CompilerParams(dimension_semantics=sem, vmem_limit_bytes=VMEM_LIMIT)


def _resident(shape):
    nd = len(shape)
    return pl.BlockSpec(shape, lambda *_: (0,) * nd, pipeline_mode=pl.Buffered(1))


def _ada_kernel(c_ref, w_ref, b_ref, o_ref):
    ca = _silu(c_ref[...]).astype(BF16)
    o_ref[...] = _dot(ca, w_ref[...].astype(BF16)) + b_ref[...]


def _ada_mod(c, ada_w, ada_b):
    L, D, N = ada_w.shape
    B = c.shape[0]
    out = pl.pallas_call(
        _ada_kernel,
        grid=(L, N // D),
        in_specs=[
            pl.BlockSpec((B, D), lambda l, j: (0, 0)),
            pl.BlockSpec((None, D, D), lambda l, j: (l, 0, j)),
            pl.BlockSpec((None, 1, D), lambda l, j: (l, 0, j)),
        ],
        out_specs=pl.BlockSpec((None, B, D), lambda l, j: (l, 0, j)),
        out_shape=jax.ShapeDtypeStruct((L, B, N), F32),
        compiler_params=_params("arbitrary", "arbitrary"),
        name="ada_mod",
    )(c, ada_w, ada_b.reshape(L, 1, N))
    return out.reshape(L, B, N_MOD, D)


def _mixer_in_kernel(x_ref, mod_ref, ng_ref, w_ref, b_ref, lng_ref, lnb_ref,
                     ua_ref, p_ref, su_ref, sv_ref, g0_ref, g1_ref, g2_ref):
    D = D_MODEL
    h = _ada_rmsnorm(x_ref[...], ng_ref[...], mod_ref[0, 1:2, :], mod_ref[0, 0:1, :])
    hb = h.astype(BF16)

    def proj(j):
        return _dot(hb, w_ref[:, j * D:(j + 1) * D]) + b_ref[:, j * D:(j + 1) * D]

    ua_ref[...] = (proj(0) * _sigmoid(proj(1))).astype(BF16)
    p_ref[...] = proj(2).astype(BF16)
    su_ref[...] = _gelu_tanh(proj(3)).astype(BF16)
    sv_ref[...] = _layernorm(_gelu_tanh(proj(4)), lng_ref[...], lnb_ref[...]).astype(BF16)
    g0_ref[...] = _sigmoid(proj(5)).astype(BF16)
    g1_ref[...] = _sigmoid(proj(6)).astype(BF16)
    g2_ref[...] = _sigmoid(proj(7)).astype(BF16)


def _mixer_in(x2, mod, norm_g, w_in, b_in, sgu_ln_g, sgu_ln_b, seq):
    T, D = x2.shape
    tm = TM_IN
    tiles_per_seq = seq // tm
    row = pl.BlockSpec((tm, D), lambda i: (i, 0))
    outs = pl.pallas_call(
        _mixer_in_kernel,
        grid=(T // tm,),
        in_specs=[
            row,
            pl.BlockSpec((1, N_MOD, D), lambda i: (i // tiles_per_seq, 0, 0)),
            _resident((1, D)),
            _resident(w_in.shape),
            _resident((1, w_in.shape[1])),
            _resident((1, D)),
            _resident((1, D)),
        ],
        out_specs=[row] * 7,
        out_shape=[jax.ShapeDtypeStruct((T, D), BF16)] * 7,
        compiler_params=_params("arbitrary"),
        name="mixer_in",
    )(x2, mod, norm_g.reshape(1, D), w_in, b_in.reshape(1, -1),
      sgu_ln_g.reshape(1, D), sgu_ln_b.reshape(1, D))
    return outs


def _mixer_mid_kernel(x_ref, mod_ref, ua_ref, p_ref, su_ref, sv_ref, g0_ref, g1_ref, g2_ref,
                      cw_ref, cb_ref, clg_ref, clb_ref, pw_ref, pb_ref,
                      plw_ref, plb_ref, pls_ref, sw_ref, sbt_ref, wo_ref,
                      o_ref, u_ext, p_ext, cv_ref, mg_ref, q_ext):
    D = D_MODEL
    ts = TS_MIX
    s_idx = pl.program_id(1)

    @pl.when(s_idx == 0)
    def _():
        u_ext[0:CONV_HALO, :] = jnp.zeros((CONV_HALO, D), F32)
        p_ext[0:POOL_HALO, :] = jnp.zeros((POOL_HALO, D), F32)

    @pl.when(s_idx > 0)
    def _():
        u_ext[0:CONV_HALO, :] = u_ext[ts:ts + CONV_HALO, :]
        p_ext[0:POOL_HALO, :] = p_ext[ts:ts + POOL_HALO, :]

    u_ext[CONV_HALO:CONV_HALO + ts, :] = ua_ref[...].astype(F32)
    p_ext[POOL_HALO:POOL_HALO + ts, :] = p_ref[...].astype(F32)

    ri = lax.broadcasted_iota(jnp.int32, (SGU_CHUNK, SGU_CHUNK), 0)
    ci = lax.broadcasted_iota(jnp.int32, (SGU_CHUNK, SGU_CHUNK), 1)
    causal = ri >= ci
    for hd in range(N_SGU_HEADS):
        cols = slice(hd * SGU_CHUNK, (hd + 1) * SGU_CHUNK)
        wm = jnp.where(causal, sw_ref[hd], 0.0).astype(BF16)
        bcol = sbt_ref[:, hd:hd + 1]
        for n in range(ts // SGU_CHUNK):
            rows = slice(n * SGU_CHUNK, (n + 1) * SGU_CHUNK)
            mixed = _dot(wm, sv_ref[rows, cols]) + bcol
            mg_ref[rows, cols] = (g2_ref[rows, cols].astype(F32)
                                  * su_ref[rows, cols].astype(F32) * mixed)

    pos1 = (s_idx * ts + lax.broadcasted_iota(jnp.int32, (ts, 1), 0) + 1).astype(F32)
    for gi, w in enumerate(POOL_WINDOWS):
        cols = slice(gi * POOL_GROUP, (gi + 1) * POOL_GROUP)
        cur = p_ext[POOL_HALO:POOL_HALO + ts, cols]
        read = lambda lo, n, cols=cols: p_ext[lo:lo + n, cols]
        k, lo = 1, SUBLANES
        while True:
            n = ts + POOL_HALO - lo
            nxt = read(lo, n) + read(lo - k, n)
            k *= 2
            if k == w:
                win = nxt[POOL_HALO - lo:, :]
                break
            lvl = q_ext.at[(lo // SUBLANES) % 2]
            lvl[lo:lo + n, :] = nxt
            read = lambda lo, n, lvl=lvl: lvl[lo:lo + n, :]
            lo += SUBLANES
        d = win * (1.0 / jnp.minimum(pos1, float(w))) - cur
        yb = (_dot(d.astype(BF16), plw_ref[gi]) + plb_ref[:, cols]) * pls_ref[:, cols]
        mg_ref[:, cols] += g1_ref[:, cols].astype(F32) * yb

    rc = CONV_ROWS
    n_a = -(-CONV_WIDTH // SUBLANES)

    def conv_rows(ci, carry):
        r0 = pl.multiple_of(ci * rc, rc)
        for ct in range(D // LANES):
            cols = slice(ct * LANES, (ct + 1) * LANES)
            acc = jnp.zeros((rc, LANES), F32) + cb_ref[:, cols]
            for r in range(SUBLANES):
                v = None
                for a in range(n_a):
                    j = SUBLANES * a + r
                    if j >= CONV_WIDTH:
                        continue
                    start = r0 + (CONV_HALO - SUBLANES - SUBLANES * a)
                    term = (u_ext[pl.ds(start, rc + SUBLANES), cols]
                            * cw_ref[CONV_WIDTH - 1 - j:CONV_WIDTH - j, cols])
                    v = term if v is None else v + term
                acc = acc + v[SUBLANES - r:SUBLANES - r + rc, :]
            cv_ref[pl.ds(r0, rc), cols] = acc
        return carry

    lax.fori_loop(0, ts // rc, conv_rows, 0)
    cn = _silu(_layernorm(cv_ref[...], clg_ref[...], clb_ref[...])).astype(BF16)
    ya = _dot(cn, pw_ref[...]) + pb_ref[...]
    merged = mg_ref[...] + g0_ref[...].astype(F32) * ya

    y = _dot(merged.astype(BF16), wo_ref[...])
    o_ref[...] = x_ref[...] + mod_ref[0, 2:3, :] * y


def _mixer_mid(x2, mod, parts, conv_w, conv_b, conv_ln_g, conv_ln_b, conv_pw_w, conv_pw_b,
               pool_w, pool_b, pool_scale, sgu_w, sgu_b, w_out, batch, seq):
    T, D = x2.shape
    ts = TS_MIX
    ns = seq // ts
    row = pl.BlockSpec((ts, D), lambda b, s: (b * ns + s, 0))
    vec = lambda a: a.reshape(1, D)
    return pl.pallas_call(
        _mixer_mid_kernel,
        grid=(batch, ns),
        in_specs=[
            row,
            pl.BlockSpec((1, N_MOD, D), lambda b, s: (b, 0, 0)),
        ] + [row] * 7 + [
            _resident((CONV_WIDTH, D)), _resident((1, D)), _resident((1, D)), _resident((1, D)),
            _resident((D, D)), _resident((1, D)),
            _resident(pool_w.shape), _resident((1, D)), _resident((1, D)),
            _resident(sgu_w.shape), _resident((SGU_CHUNK, N_SGU_HEADS)),
            _resident((D, D)),
        ],
        out_specs=row,
        out_shape=jax.ShapeDtypeStruct((T, D), F32),
        scratch_shapes=[
            pltpu.VMEM((ts + CONV_HALO, D), F32),
            pltpu.VMEM((ts + POOL_HALO, D), F32),
            pltpu.VMEM((ts, D), F32),
            pltpu.VMEM((ts, D), F32),
            pltpu.VMEM((2, ts + POOL_HALO, POOL_GROUP), F32),
        ],
        compiler_params=_params("arbitrary", "arbitrary"),
        name="mixer_mid",
    )(x2, mod, *parts, conv_w, vec(conv_b), vec(conv_ln_g), vec(conv_ln_b), conv_pw_w,
      vec(conv_pw_b), pool_w, vec(pool_b), vec(pool_scale), sgu_w, sgu_b.T, w_out)


def _ffn_kernel(x_ref, mod_ref, ng_ref, w1_ref, w3_ref, w2_ref, o_ref):
    x = x_ref[...]
    hb = _ada_rmsnorm(x, ng_ref[...], mod_ref[0, 4:5, :], mod_ref[0, 3:4, :]).astype(BF16)
    act = (_silu(_dot(hb, w1_ref[...])) * _dot(hb, w3_ref[...])).astype(BF16)
    o_ref[...] = x + mod_ref[0, 5:6, :] * _dot(act, w2_ref[...])


def _ffn(x2, mod, norm_g, w1, w3, w2, seq):
    T, D = x2.shape
    tm = TM_FFN
    tiles_per_seq = seq // tm
    row = pl.BlockSpec((tm, D), lambda i: (i, 0))
    return pl.pallas_call(
        _ffn_kernel,
        grid=(T // tm,),
        in_specs=[
            row,
            pl.BlockSpec((1, N_MOD, D), lambda i: (i // tiles_per_seq, 0, 0)),
            _resident((1, D)),
            _resident(w1.shape),
            _resident(w3.shape),
            _resident(w2.shape),
        ],
        out_specs=row,
        out_shape=jax.ShapeDtypeStruct((T, D), F32),
        compiler_params=_params("arbitrary"),
        name="ffn",
    )(x2, mod, norm_g.reshape(1, D), w1, w3, w2)


def _split_bf16(v):
    hi = v.astype(BF16)
    lo = (v - hi.astype(F32)).astype(BF16)
    return hi, lo


def _router_kernel(x_ref, mod_ref, ng_ref, rw_ref, h_ref, idx_ref, gate_ref):
    tm = x_ref.shape[0]
    h = _ada_rmsnorm(x_ref[...], ng_ref[...], mod_ref[0, 4:5, :], mod_ref[0, 3:4, :])
    for s in range(ROW_TILES):
        h_ref[pl.ds(s, tm, stride=ROW_TILES), :] = h[:, s * LANES:(s + 1) * LANES]

    h_hi, h_lo = _split_bf16(h)
    w_hi, w_lo = _split_bf16(rw_ref[...])
    logits = _dot(h_hi, w_hi) + (_dot(h_lo, w_hi) + _dot(h_hi, w_lo))

    lane = lax.broadcasted_iota(jnp.int32, (tm, LANES), 1)
    neg = jnp.float32(-jnp.inf)
    logits = jnp.where(lane < N_EXPERTS, logits, neg)
    m1 = jnp.max(logits, axis=-1, keepdims=True)
    i1 = jnp.min(jnp.where(logits == m1, lane, LANES), axis=-1, keepdims=True)
    rest = jnp.where(lane == i1, neg, logits)
    m2 = jnp.max(rest, axis=-1, keepdims=True)
    i2 = jnp.min(jnp.where(rest == m2, lane, LANES), axis=-1, keepdims=True)
    e2 = jnp.exp(m2 - m1)
    den = 1.0 + e2
    idx_ref[...] = jnp.where(lane == 0, i1, i2)[:, :TOP_K]
    gate_ref[...] = jnp.where(lane == 0, 1.0 / den, e2 / den)[:, :TOP_K]


def _router(x2, mod, norm_g, router_w, seq):
    T, D = x2.shape
    tm = TM_ROUTE
    tiles_per_seq = seq // tm
    rw = jnp.zeros((D, LANES), F32).at[:, :N_EXPERTS].set(router_w)
    return pl.pallas_call(
        _router_kernel,
        grid=(T // tm,),
        in_specs=[
            pl.BlockSpec((tm, D), lambda i: (i, 0)),
            pl.BlockSpec((1, N_MOD, D), lambda i: (i // tiles_per_seq, 0, 0)),
            _resident((1, D)),
            _resident((D, LANES)),
        ],
        out_specs=[
            pl.BlockSpec((tm * ROW_TILES, LANES), lambda i: (i, 0)),
            pl.BlockSpec((tm, TOP_K), lambda i: (i, 0)),
            pl.BlockSpec((tm, TOP_K), lambda i: (i, 0)),
        ],
        out_shape=[
            jax.ShapeDtypeStruct((T * ROW_TILES, LANES), F32),
            jax.ShapeDtypeStruct((T, TOP_K), jnp.int32),
            jax.ShapeDtypeStruct((T, TOP_K), F32),
        ],
        compiler_params=_params("arbitrary"),
        name="router",
    )(x2, mod, norm_g.reshape(1, D), rw)


def _dispatch_plan(top_idx, n_tokens):
    bm = BM_MOE
    n_flat = n_tokens * TOP_K
    n_blocks = -(-(n_flat + N_EXPERTS * (bm - 1)) // bm)
    n_slots = n_blocks * bm
    flat_e = top_idx.reshape(-1)
    order = jnp.argsort(flat_e, stable=True).astype(jnp.int32)
    counts = jnp.sum(flat_e[:, None] == jnp.arange(N_EXPERTS, dtype=jnp.int32)[None, :],
                     axis=0, dtype=jnp.int32)
    padded = (counts + bm - 1) // bm * bm
    start_sorted = jnp.cumsum(counts) - counts
    ends_padded = jnp.cumsum(padded)
    start_padded = ends_padded - padded
    n_used = (ends_padded[-1] // bm).astype(jnp.int32)
    block_start = jnp.arange(n_blocks, dtype=jnp.int32) * bm
    block_expert = jnp.minimum(jnp.searchsorted(ends_padded, block_start, side="right"),
                               N_EXPERTS - 1).astype(jnp.int32)
    last_e = block_expert[jnp.maximum(n_used - 1, 0)]
    block_expert = jnp.where(jnp.arange(n_blocks) < n_used, block_expert, last_e)
    slot = jnp.arange(n_slots, dtype=jnp.int32)
    slot_e = jnp.repeat(block_expert, bm)
    rank = slot - start_padded[slot_e]
    valid = (rank < counts[slot_e]) & (slot < ends_padded[-1])
    flat = order[jnp.clip(start_sorted[slot_e] + rank, 0, n_flat - 1)]
    spare = n_flat + jnp.cumsum(jnp.logical_not(valid).astype(jnp.int32)) - 1
    slot_src = jnp.where(valid, flat // TOP_K, 0).astype(jnp.int32)
    slot_dst = jnp.where(valid, (flat % TOP_K) * n_tokens + flat // TOP_K, spare).astype(jnp.int32)
    return (slot_src.reshape(n_blocks, 1, bm), slot_dst.reshape(n_blocks, 1, bm),
            block_expert, n_used.reshape(1), n_blocks)


def _moe_kernel(nf, be_ref, nu_ref, src_cur, src_nxt, dst_cur, dst_prv, h_hbm, w1_ref, w3_ref,
                w2_ref, y_hbm, xbuf, xs_ref, acc_ref, ybuf, gsem, ssem):
    bm = BM_MOE
    b = pl.program_id(0)
    f = pl.program_id(1)
    n_used = nu_ref[0]
    slot = b % 2

    rt = ROW_TILES
    blk_rows = bm * rt
    per_step = bm // nf

    def token_rows(ref, tok):
        return ref.at[pl.ds(pl.multiple_of(tok * rt, rt), rt), :]

    def gather_copy(idx_ref, buf_slot, r):
        return pltpu.make_async_copy(token_rows(h_hbm, idx_ref[0, 0, r]),
                                     token_rows(xbuf, buf_slot * bm + r), gsem.at[buf_slot])

    def scatter_copy(idx_ref, r):
        return pltpu.make_async_copy(token_rows(ybuf, r), token_rows(y_hbm, idx_ref[0, 0, r]),
                                     ssem.at[0])

    def wait_gather(buf_slot):
        pltpu.make_async_copy(h_hbm.at[pl.ds(0, blk_rows), :],
                              xbuf.at[pl.ds(pl.multiple_of(buf_slot * blk_rows, blk_rows), blk_rows), :],
                              gsem.at[buf_slot]).wait()

    def wait_scatter():
        pltpu.make_async_copy(ybuf, y_hbm.at[pl.ds(0, blk_rows), :], ssem.at[0]).wait()

    def issue_piece(piece, n_pieces):
        lo = per_step * piece // n_pieces
        hi = per_step * (piece + 1) // n_pieces
        for k in range(lo, hi):
            r = f * per_step + k
            gather_copy(src_nxt, 1 - slot, r).start()
            scatter_copy(dst_prv, r).start()

    @pl.when(b < n_used)
    def _():
        @pl.when(f == 0)
        def _():
            @pl.when(b == 0)
            def _():
                def body(r, carry):
                    gather_copy(src_cur, 0, r).start()
                    return carry
                lax.fori_loop(0, bm, body, 0, unroll=DMA_UNROLL)
                ybuf[...] = jnp.zeros_like(ybuf)

            wait_gather(slot)
            base = slot * blk_rows
            for s in range(rt):
                xs_ref[:, s * LANES:(s + 1) * LANES] = (
                    xbuf[pl.ds(base + s, bm, stride=rt), :].astype(BF16))
            acc_ref[...] = jnp.zeros_like(acc_ref)

        xb = xs_ref[...]
        issue_piece(0, 3)
        h1 = _dot(xb, w1_ref[...])
        issue_piece(1, 3)
        h3 = _dot(xb, w3_ref[...])
        issue_piece(2, 3)
        act = (_silu(h1) * h3).astype(BF16)
        acc_ref[...] += _dot(act, w2_ref[...])

        @pl.when(f == nf - 1)
        def _():
            wait_scatter()
            for s in range(rt):
                ybuf[pl.ds(s, bm, stride=rt), :] = acc_ref[:, s * LANES:(s + 1) * LANES]

            @pl.when(b == n_used - 1)
            def _():
                def body(r, carry):
                    scatter_copy(dst_cur, r).start()
                    return carry
                lax.fori_loop(0, bm, body, 0, unroll=DMA_UNROLL)
                wait_scatter()
                wait_gather(1 - slot)

    @pl.when(jnp.logical_and(b >= n_used, f == pl.num_programs(1) - 1))
    def _():
        ybuf[...] = jnp.zeros_like(ybuf)
        fill = pltpu.make_async_copy(
            ybuf, y_hbm.at[pl.ds(pl.multiple_of(b * blk_rows, blk_rows), blk_rows), :], ssem.at[0])
        fill.start()
        fill.wait()


def _moe_experts(h3, slot_src, slot_dst, block_expert, n_used, n_blocks, w1, w3, w2):
    D = D_MODEL
    bm, tf = BM_MOE, TF_MOE
    F = w1.shape[2]
    nf = F // tf
    n_slots = n_blocks * bm

    def fsel(b, f, nu):
        return jnp.where(b < nu[0], f, nf - 1)

    smem_blk = lambda imap: pl.BlockSpec((1, 1, bm), imap, memory_space=pltpu.SMEM)
    grid_spec = pltpu.PrefetchScalarGridSpec(
        num_scalar_prefetch=2,
        grid=(n_blocks, nf),
        in_specs=[
            smem_blk(lambda b, f, be, nu: (b, 0, 0)),
            smem_blk(lambda b, f, be, nu: (jnp.minimum(b + 1, n_blocks - 1), 0, 0)),
            smem_blk(lambda b, f, be, nu: (b, 0, 0)),
            smem_blk(lambda b, f, be, nu: (jnp.maximum(b - 1, 0), 0, 0)),
            pl.BlockSpec(memory_space=pl.ANY),
            pl.BlockSpec((None, D, tf), lambda b, f, be, nu: (be[b], 0, fsel(b, f, nu))),
            pl.BlockSpec((None, D, tf), lambda b, f, be, nu: (be[b], 0, fsel(b, f, nu))),
            pl.BlockSpec((None, tf, D), lambda b, f, be, nu: (be[b], fsel(b, f, nu), 0)),
        ],
        out_specs=pl.BlockSpec(memory_space=pl.ANY),
        scratch_shapes=[
            pltpu.VMEM((2 * bm * ROW_TILES, LANES), F32),
            pltpu.VMEM((bm, D), BF16),
            pltpu.VMEM((bm, D), F32),
            pltpu.VMEM((bm * ROW_TILES, LANES), F32),
            pltpu.SemaphoreType.DMA((2,)),
            pltpu.SemaphoreType.DMA((1,)),
        ],
    )
    return pl.pallas_call(
        functools.partial(_moe_kernel, nf),
        grid_spec=grid_spec,
        out_shape=jax.ShapeDtypeStruct((n_slots * ROW_TILES, LANES), F32),
        compiler_params=_params("arbitrary", "arbitrary"),
        name="moe_experts",
    )(block_expert, n_used, slot_src, slot_src, slot_dst, slot_dst, h3, w1, w3, w2)


def _combine_kernel(x_ref, mod_ref, y0_ref, y1_ref, gate_ref, fg_ref, o_ref, f_ref):
    tm = x_ref.shape[0]
    g = gate_ref[...]
    for s in range(ROW_TILES):
        rows = pl.ds(s, tm, stride=ROW_TILES)
        f_ref[:, s * LANES:(s + 1) * LANES] = g[:, 0:1] * y0_ref[rows, :] + g[:, 1:2] * y1_ref[rows, :]
    x = x_ref[...] + mod_ref[0, 5:6, :] * f_ref[...]
    o_ref[...] = (x * lax.rsqrt(jnp.mean(x * x, axis=-1, keepdims=True) + EPS)) * fg_ref[...]


def _combine(x2, mod, y_slots, gates, final_g, seq):
    T, D = x2.shape
    tm = TM_OUT
    tiles_per_seq = seq // tm
    n_tiles = T // tm
    row = pl.BlockSpec((tm, D), lambda i: (i, 0))
    return pl.pallas_call(
        _combine_kernel,
        grid=(T // tm,),
        in_specs=[
            row,
            pl.BlockSpec((1, N_MOD, D), lambda i: (i // tiles_per_seq, 0, 0)),
            pl.BlockSpec((tm * ROW_TILES, LANES), lambda i: (i, 0)),
            pl.BlockSpec((tm * ROW_TILES, LANES), lambda i: (n_tiles + i, 0)),
            pl.BlockSpec((tm, TOP_K), lambda i: (i, 0)),
            _resident((1, D)),
        ],
        out_specs=row,
        out_shape=jax.ShapeDtypeStruct((T, D), F32),
        scratch_shapes=[pltpu.VMEM((tm, D), F32)],
        compiler_params=_params("arbitrary"),
        name="combine_norm",
    )(x2, mod, y_slots, y_slots, gates, final_g.reshape(1, D))


def _mixer_layer(x2, mod, i, batch, seq, p):
    parts = _mixer_in(x2, mod, p["norm1_g"][i], p["w_in"][i].astype(BF16), p["b_in"][i],
                      p["sgu_ln_g"][i], p["sgu_ln_b"][i], seq)
    return _mixer_mid(x2, mod, parts, p["conv_w"][i], p["conv_b"][i], p["conv_ln_g"][i],
                      p["conv_ln_b"][i], p["conv_pw_w"][i].astype(BF16), p["conv_pw_b"][i],
                      p["pool_w"][i].astype(BF16), p["pool_b"][i], p["pool_scale"][i],
                      p["sgu_w"][i], p["sgu_b"][i], p["w_out"][i].astype(BF16), batch, seq)


def kernel(x, c, ada_w, ada_b, norm1_g, norm2_g, w_in, b_in, conv_w, conv_b, conv_ln_g, conv_ln_b, conv_pw_w, conv_pw_b, pool_w, pool_b, pool_scale, sgu_ln_g, sgu_ln_b, sgu_w, sgu_b, w_out, ffn_w1, ffn_w3, ffn_w2, router_w, moe_w1, moe_w3, moe_w2, final_norm_g):
    p = dict(norm1_g=norm1_g, w_in=w_in, b_in=b_in, conv_w=conv_w, conv_b=conv_b,
             conv_ln_g=conv_ln_g, conv_ln_b=conv_ln_b, conv_pw_w=conv_pw_w, conv_pw_b=conv_pw_b,
             pool_w=pool_w, pool_b=pool_b, pool_scale=pool_scale, sgu_ln_g=sgu_ln_g,
             sgu_ln_b=sgu_ln_b, sgu_w=sgu_w, sgu_b=sgu_b, w_out=w_out)
    batch, seq, D = x.shape
    T = batch * seq
    mod = _ada_mod(c, ada_w, ada_b)
    x2 = x.reshape(T, D)

    x2 = _mixer_layer(x2, mod[0], 0, batch, seq, p)
    x2 = _ffn(x2, mod[0], norm2_g[0], ffn_w1[0].astype(BF16), ffn_w3[0].astype(BF16),
              ffn_w2[0].astype(BF16), seq)

    x2 = _mixer_layer(x2, mod[1], 1, batch, seq, p)
    h3, top_idx, gates = _router(x2, mod[1], norm2_g[1], router_w[0], seq)
    slot_src, slot_dst, block_expert, n_used, n_blocks = _dispatch_plan(top_idx, T)
    y_slots = _moe_experts(h3, slot_src, slot_dst, block_expert, n_used, n_blocks,
                           moe_w1[0].astype(BF16), moe_w3[0].astype(BF16), moe_w2[0].astype(BF16))
    out = _combine(x2, mod[1], y_slots, gates, final_norm_g, seq)
    return out.reshape(batch, seq, D)
```

```python
import functools
import math

import jax
import jax.numpy as jnp
from jax import lax
from jax.experimental import pallas as pl
from jax.experimental.pallas import tpu as pltpu

F32 = jnp.float32
BF16 = jnp.bfloat16

D_MODEL = 1024
CONV_WIDTH = 31
POOL_WINDOWS = (2, 4, 8, 16)
POOL_GROUP = D_MODEL // len(POOL_WINDOWS)
SGU_CHUNK = 128
N_SGU_HEADS = 8
N_EXPERTS = 8
TOP_K = 2
N_MOD = 6
N_PARTS = 7
EPS = 1e-6

LANES = 128
SUBLANES = 8
ROW_TILES = D_MODEL // LANES

CONV_HALO = 32
POOL_HALO = 32
CONV_ROWS = 128

VMEM_LIMIT = 56 * 1024 * 1024

TM_IN = 512
TS_MIX = 512
TM_FFN = 512
TM_ROUTE = 512
BM_MOE = 512
TF_MOE = 1792
TM_OUT = 512
DMA_UNROLL = 8


def _sigmoid(x):
    return 0.5 * (1.0 + jnp.tanh(0.5 * x))


def _silu(x):
    return x * _sigmoid(x)


def _gelu_tanh(x):
    c = math.sqrt(2.0 / math.pi)
    return x * (0.5 * (1.0 + jnp.tanh(c * (x + 0.044715 * (x * x * x)))))


def _dot(a, b):
    return jnp.dot(a, b, preferred_element_type=F32)


def _layernorm(x, g, b):
    mu = jnp.mean(x, axis=-1, keepdims=True)
    xc = x - mu
    var = jnp.mean(xc * xc, axis=-1, keepdims=True)
    return xc * lax.rsqrt(var + EPS) * g + b


def _ada_rmsnorm(x, g, scale, shift):
    y = x * lax.rsqrt(jnp.mean(x * x, axis=-1, keepdims=True) + EPS)
    return (y * g) * (1.0 + scale) + shift


def _params(*sem):
    return pltpu.CompilerParams(dimension_semantics=sem, vmem_limit_bytes=VMEM_LIMIT)


def _resident(shape):
    nd = len(shape)
    return pl.BlockSpec(shape, lambda *_: (0,) * nd, pipeline_mode=pl.Buffered(1))


def _ada_kernel(c_ref, w_ref, b_ref, o_ref):
    ca = _silu(c_ref[...]).astype(BF16)
    o_ref[...] = _dot(ca, w_ref[...].astype(BF16)) + b_ref[...]


def _ada_mod(c, ada_w, ada_b):
    L, D, N = ada_w.shape
    B = c.shape[0]
    out = pl.pallas_call(
        _ada_kernel,
        grid=(L, N // D),
        in_specs=[
            pl.BlockSpec((B, D), lambda l, j: (0, 0)),
            pl.BlockSpec((None, D, D), lambda l, j: (l, 0, j)),
            pl.BlockSpec((None, 1, D), lambda l, j: (l, 0, j)),
        ],
        out_specs=pl.BlockSpec((None, B, D), lambda l, j: (l, 0, j)),
        out_shape=jax.ShapeDtypeStruct((L, B, N), F32),
        compiler_params=_params("arbitrary", "arbitrary"),
        name="ada_mod",
    )(c, ada_w, ada_b.reshape(L, 1, N))
    return out.reshape(L, B, N_MOD, D)


def _part_views(parts_ref):
    return [parts_ref.at[:, j * D_MODEL:(j + 1) * D_MODEL] for j in range(N_PARTS)]


def _mixer_in_kernel(x_ref, mod_ref, ng_ref, w_ref, b_ref, lng_ref, lnb_ref, parts_ref):
    D = D_MODEL
    ua_ref, p_ref, su_ref, sv_ref, g0_ref, g1_ref, g2_ref = _part_views(parts_ref)
    h = _ada_rmsnorm(x_ref[...], ng_ref[...], mod_ref[0, 1:2, :], mod_ref[0, 0:1, :])
    hb = h.astype(BF16)

    def proj_pair(j):
        z = _dot(hb, w_ref[:, j * D:(j + 2) * D]) + b_ref[:, j * D:(j + 2) * D]
        return z[:, :D], z[:, D:]

    a, g = proj_pair(0)
    ua_ref[...] = (a * _sigmoid(g)).astype(BF16)
    p, su = proj_pair(2)
    p_ref[...] = p.astype(BF16)
    su_ref[...] = _gelu_tanh(su).astype(BF16)
    sv, g0 = proj_pair(4)
    sv_ref[...] = _layernorm(_gelu_tanh(sv), lng_ref[...], lnb_ref[...]).astype(BF16)
    g0_ref[...] = _sigmoid(g0).astype(BF16)
    g1, g2 = proj_pair(6)
    g1_ref[...] = _sigmoid(g1).astype(BF16)
    g2_ref[...] = _sigmoid(g2).astype(BF16)


def _mixer_in(x2, mod, norm_g, w_in, b_in, sgu_ln_g, sgu_ln_b, seq):
    T, D = x2.shape
    tm = TM_IN
    tiles_per_seq = seq // tm
    return pl.pallas_call(
        _mixer_in_kernel,
        grid=(T // tm,),
        in_specs=[
            pl.BlockSpec((tm, D), lambda i: (i, 0)),
            pl.BlockSpec((1, N_MOD, D), lambda i: (i // tiles_per_seq, 0, 0)),
            _resident((1, D)),
            _resident(w_in.shape),
            _resident((1, w_in.shape[1])),
            _resident((1, D)),
            _resident((1, D)),
        ],
        out_specs=pl.BlockSpec((tm, N_PARTS * D), lambda i: (i, 0)),
        out_shape=jax.ShapeDtypeStruct((T, N_PARTS * D), BF16),
        compiler_params=_params("arbitrary"),
        name="mixer_in",
    )(x2, mod, norm_g.reshape(1, D), w_in, b_in.reshape(1, -1),
      sgu_ln_g.reshape(1, D), sgu_ln_b.reshape(1, D))


def _mixer_mid_kernel(x_ref, mod_ref, parts_ref,
                      cw_ref, cb_ref, clg_ref, clb_ref, pw_ref, pb_ref,
                      plw_ref, plb_ref, pls_ref, sw_ref, sbt_ref, wo_ref,
                      o_ref, u_ext, p_ext, cv_ref, mg_ref, q_ext):
    D = D_MODEL
    ts = TS_MIX
    ua_ref, p_ref, su_ref, sv_ref, g0_ref, g1_ref, g2_ref = _part_views(parts_ref)
    s_idx = pl.program_id(1)

    @pl.when(s_idx == 0)
    def _():
        u_ext[0:CONV_HALO, :] = jnp.zeros((CONV_HALO, D), F32)
        p_ext[0:POOL_HALO, :] = jnp.zeros((POOL_HALO, D), F32)

    @pl.when(s_idx > 0)
    def _():
        u_ext[0:CONV_HALO, :] = u_ext[ts:ts + CONV_HALO, :]
        p_ext[0:POOL_HALO, :] = p_ext[ts:ts + POOL_HALO, :]

    u_ext[CONV_HALO:CONV_HALO + ts, :] = ua_ref[...].astype(F32)
    p_ext[POOL_HALO:POOL_HALO + ts, :] = p_ref[...].astype(F32)

    ri = lax.broadcasted_iota(jnp.int32, (SGU_CHUNK, SGU_CHUNK), 0)
    ci = lax.broadcasted_iota(jnp.int32, (SGU_CHUNK, SGU_CHUNK), 1)
    causal = ri >= ci
    for hd in range(N_SGU_HEADS):
        cols = slice(hd * SGU_CHUNK, (hd + 1) * SGU_CHUNK)
        wm = jnp.where(causal, sw_ref[hd], 0.0).astype(BF16)
        bcol = sbt_ref[:, hd:hd + 1]
        for n in range(ts // SGU_CHUNK):
            rows = slice(n * SGU_CHUNK, (n + 1) * SGU_CHUNK)
            mixed = _dot(wm, sv_ref[rows, cols]) + bcol
            mg_ref[rows, cols] = (g2_ref[rows, cols].astype(F32)
                                  * su_ref[rows, cols].astype(F32) * mixed)

    pos1 = (s_idx * ts + lax.broadcasted_iota(jnp.int32, (ts, 1), 0) + 1).astype(F32)
    for gi, w in enumerate(POOL_WINDOWS):
        cols = slice(gi * POOL_GROUP, (gi + 1) * POOL_GROUP)
        cur = p_ext[POOL_HALO:POOL_HALO + ts, cols]
        read = lambda lo, n, cols=cols: p_ext[lo:lo + n, cols]
        k, lo = 1, SUBLANES
        while True:
            n = ts + POOL_HALO - lo
            nxt = read(lo, n) + read(lo - k, n)
            k *= 2
            if k == w:
                win = nxt[POOL_HALO - lo:, :]
                break
            lvl = q_ext.at[(lo // SUBLANES) % 2]
            lvl[lo:lo + n, :] = nxt
            read = lambda lo, n, lvl=lvl: lvl[lo:lo + n, :]
            lo += SUBLANES
        d = win * (1.0 / jnp.minimum(pos1, float(w))) - cur
        yb = (_dot(d.astype(BF16), plw_ref[gi]) + plb_ref[:, cols]) * pls_ref[:, cols]
        mg_ref[:, cols] += g1_ref[:, cols].astype(F32) * yb

    rc = CONV_ROWS
    n_a = -(-CONV_WIDTH // SUBLANES)

    def conv_rows(ci, carry):
        r0 = pl.multiple_of(ci * rc, rc)
        for ct in range(D // LANES):
            cols = slice(ct * LANES, (ct + 1) * LANES)
            acc = jnp.zeros((rc, LANES), F32) + cb_ref[:, cols]
            for r in range(SUBLANES):
                v = None
                for a in range(n_a):
                    j = SUBLANES * a + r
                    if j >= CONV_WIDTH:
                        continue
                    start = r0 + (CONV_HALO - SUBLANES - SUBLANES * a)
                    term = (u_ext[pl.ds(start, rc + SUBLANES), cols]
                            * cw_ref[CONV_WIDTH - 1 - j:CONV_WIDTH - j, cols])
                    v = term if v is None else v + term
                acc = acc + v[SUBLANES - r:SUBLANES - r + rc, :]
            cv_ref[pl.ds(r0, rc), cols] = acc
        return carry

    lax.fori_loop(0, ts // rc, conv_rows, 0)
    cn = _silu(_layernorm(cv_ref[...], clg_ref[...], clb_ref[...])).astype(BF16)
    ya = _dot(cn, pw_ref[...]) + pb_ref[...]
    merged = mg_ref[...] + g0_ref[...].astype(F32) * ya

    y = _dot(merged.astype(BF16), wo_ref[...])
    o_ref[...] = x_ref[...] + mod_ref[0, 2:3, :] * y


def _mixer_mid(x2, mod, parts, conv_w, conv_b, conv_ln_g, conv_ln_b, conv_pw_w, conv_pw_b,
               pool_w, pool_b, pool_scale, sgu_w, sgu_b, w_out, batch, seq):
    T, D = x2.shape
    ts = TS_MIX
    ns = seq // ts
    row = pl.BlockSpec((ts, D), lambda b, s: (b * ns + s, 0))
    vec = lambda a: a.reshape(1, D)
    return pl.pallas_call(
        _mixer_mid_kernel,
        grid=(batch, ns),
        in_specs=[
            row,
            pl.BlockSpec((1, N_MOD, D), lambda b, s: (b, 0, 0)),
            pl.BlockSpec((ts, N_PARTS * D), lambda b, s: (b * ns + s, 0)),
        ] + [
            _resident((CONV_WIDTH, D)), _resident((1, D)), _resident((1, D)), _resident((1, D)),
            _resident((D, D)), _resident((1, D)),
            _resident(pool_w.shape), _resident((1, D)), _resident((1, D)),
            _resident(sgu_w.shape), _resident((SGU_CHUNK, N_SGU_HEADS)),
            _resident((D, D)),
        ],
        out_specs=row,
        out_shape=jax.ShapeDtypeStruct((T, D), F32),
        scratch_shapes=[
            pltpu.VMEM((ts + CONV_HALO, D), F32),
            pltpu.VMEM((ts + POOL_HALO, D), F32),
            pltpu.VMEM((ts, D), F32),
            pltpu.VMEM((ts, D), F32),
            pltpu.VMEM((2, ts + POOL_HALO, POOL_GROUP), F32),
        ],
        compiler_params=_params("arbitrary", "arbitrary"),
        name="mixer_mid",
    )(x2, mod, parts, conv_w, vec(conv_b), vec(conv_ln_g), vec(conv_ln_b), conv_pw_w,
      vec(conv_pw_b), pool_w, vec(pool_b), vec(pool_scale), sgu_w, sgu_b.T, w_out)


def _ffn_kernel(x_ref, mod_ref, ng_ref, w1_ref, w3_ref, w2_ref, o_ref):
    x = x_ref[...]
    hb = _ada_rmsnorm(x, ng_ref[...], mod_ref[0, 4:5, :], mod_ref[0, 3:4, :]).astype(BF16)
    act = (_silu(_dot(hb, w1_ref[...])) * _dot(hb, w3_ref[...])).astype(BF16)
    o_ref[...] = x + mod_ref[0, 5:6, :] * _dot(act, w2_ref[...])


def _ffn(x2, mod, norm_g, w1, w3, w2, seq):
    T, D = x2.shape
    tm = TM_FFN
    tiles_per_seq = seq // tm
    row = pl.BlockSpec((tm, D), lambda i: (i, 0))
    return pl.pallas_call(
        _ffn_kernel,
        grid=(T // tm,),
        in_specs=[
            row,
            pl.BlockSpec((1, N_MOD, D), lambda i: (i // tiles_per_seq, 0, 0)),
            _resident((1, D)),
            _resident(w1.shape),
            _resident(w3.shape),
            _resident(w2.shape),
        ],
        out_specs=row,
        out_shape=jax.ShapeDtypeStruct((T, D), F32),
        compiler_params=_params("arbitrary"),
        name="ffn",
    )(x2, mod, norm_g.reshape(1, D), w1, w3, w2)


def _split_bf16(v):
    hi = v.astype(BF16)
    lo = (v - hi.astype(F32)).astype(BF16)
    return hi, lo


def _router_kernel(x_ref, mod_ref, ng_ref, rw_ref, h_ref, idx_ref, gate_ref):
    tm = x_ref.shape[0]
    h = _ada_rmsnorm(x_ref[...], ng_ref[...], mod_ref[0, 4:5, :], mod_ref[0, 3:4, :])
    for s in range(ROW_TILES):
        h_ref[pl.ds(s, tm, stride=ROW_TILES), :] = h[:, s * LANES:(s + 1) * LANES]

    h_hi, h_lo = _split_bf16(h)
    w_hi, w_lo = _split_bf16(rw_ref[...])
    logits = _dot(h_hi, w_hi) + (_dot(h_lo, w_hi) + _dot(h_hi, w_lo))

    lane = lax.broadcasted_iota(jnp.int32, (tm, LANES), 1)
    neg = jnp.float32(-jnp.inf)
    logits = jnp.where(lane < N_EXPERTS, logits, neg)
    m1 = jnp.max(logits, axis=-1, keepdims=True)
    i1 = jnp.min(jnp.where(logits == m1, lane, LANES), axis=-1, keepdims=True)
    rest = jnp.where(lane == i1, neg, logits)
    m2 = jnp.max(rest, axis=-1, keepdims=True)
    i2 = jnp.min(jnp.where(rest == m2, lane, LANES), axis=-1, keepdims=True)
    e2 = jnp.exp(m2 - m1)
    den = 1.0 + e2
    idx_ref[...] = jnp.where(lane == 0, i1, i2)[:, :TOP_K]
    gate_ref[...] = jnp.where(lane == 0, 1.0 / den, e2 / den)[:, :TOP_K]


def _router(x2, mod, norm_g, router_w, seq):
    T, D = x2.shape
    tm = TM_ROUTE
    tiles_per_seq = seq // tm
    rw = jnp.zeros((D, LANES), F32).at[:, :N_EXPERTS].set(router_w)
    return pl.pallas_call(
        _router_kernel,
        grid=(T // tm,),
        in_specs=[
            pl.BlockSpec((tm, D), lambda i: (i, 0)),
            pl.BlockSpec((1, N_MOD, D), lambda i: (i // tiles_per_seq, 0, 0)),
            _resident((1, D)),
            _resident((D, LANES)),
        ],
        out_specs=[
            pl.BlockSpec((tm * ROW_TILES, LANES), lambda i: (i, 0)),
            pl.BlockSpec((tm, TOP_K), lambda i: (i, 0)),
            pl.BlockSpec((tm, TOP_K), lambda i: (i, 0)),
        ],
        out_shape=[
            jax.ShapeDtypeStruct((T * ROW_TILES, LANES), F32),
            jax.ShapeDtypeStruct((T, TOP_K), jnp.int32),
            jax.ShapeDtypeStruct((T, TOP_K), F32),
        ],
        compiler_params=_params("arbitrary"),
        name="router",
    )(x2, mod, norm_g.reshape(1, D), rw)


def _dispatch_plan(top_idx, n_tokens):
    bm = BM_MOE
    n_flat = n_tokens * TOP_K
    n_blocks = -(-(n_flat + N_EXPERTS * (bm - 1)) // bm)
    n_slots = n_blocks * bm
    flat_e = top_idx.reshape(-1)
    order = jnp.argsort(flat_e, stable=True).astype(jnp.int32)
    counts = jnp.sum(flat_e[:, None] == jnp.arange(N_EXPERTS, dtype=jnp.int32)[None, :],
                     axis=0, dtype=jnp.int32)
    padded = (counts + bm - 1) // bm * bm
    start_sorted = jnp.cumsum(counts) - counts
    ends_padded = jnp.cumsum(padded)
    start_padded = ends_padded - padded
    n_used = (ends_padded[-1] // bm).astype(jnp.int32)
    block_start = jnp.arange(n_blocks, dtype=jnp.int32) * bm
    block_expert = jnp.minimum(jnp.searchsorted(ends_padded, block_start, side="right"),
                               N_EXPERTS - 1).astype(jnp.int32)
    last_e = block_expert[jnp.maximum(n_used - 1, 0)]
    block_expert = jnp.where(jnp.arange(n_blocks) < n_used, block_expert, last_e)
    slot = jnp.arange(n_slots, dtype=jnp.int32)
    slot_e = jnp.repeat(block_expert, bm)
    rank = slot - start_padded[slot_e]
    valid = (rank < counts[slot_e]) & (slot < ends_padded[-1])
    flat = order[jnp.clip(start_sorted[slot_e] + rank, 0, n_flat - 1)]
    spare = n_flat + jnp.cumsum(jnp.logical_not(valid).astype(jnp.int32)) - 1
    slot_src = jnp.where(valid, flat // TOP_K, 0).astype(jnp.int32)
    slot_dst = jnp.where(valid, (flat % TOP_K) * n_tokens + flat // TOP_K, spare).astype(jnp.int32)
    return (slot_src.reshape(n_blocks, 1, bm), slot_dst.reshape(n_blocks, 1, bm),
            block_expert, n_used.reshape(1), n_blocks)


def _moe_kernel(nf, be_ref, nu_ref, idx_ref, h_hbm, w13_ref, w2_ref,
                y_hbm, xbuf, xs_ref, acc_ref, ybuf, gsem, ssem):
    bm = BM_MOE
    b = pl.program_id(0)
    f = pl.program_id(1)
    n_used = nu_ref[0]
    slot = b % 2

    rt = ROW_TILES
    blk_rows = bm * rt
    per_step = bm // nf
    tf = w2_ref.shape[0]
    src_cur, dst_cur, src_nxt, dst_prv = 0, bm, 2 * bm, 3 * bm

    def token_rows(ref, tok):
        start = tok * rt if isinstance(tok, int) else pl.multiple_of(tok * rt, rt)
        return ref.at[pl.ds(start, rt), :]

    def gather_copy(which, buf_slot, r):
        return pltpu.make_async_copy(token_rows(h_hbm, idx_ref[0, 0, which + r]),
                                     token_rows(xbuf, buf_slot * bm + r), gsem.at[buf_slot])

    def scatter_copy(which, r):
        return pltpu.make_async_copy(token_rows(ybuf, r),
                                     token_rows(y_hbm, idx_ref[0, 0, which + r]), ssem.at[0])

    def wait_gather(buf_slot):
        pltpu.make_async_copy(h_hbm.at[pl.ds(0, blk_rows), :],
                              xbuf.at[pl.ds(pl.multiple_of(buf_slot * blk_rows, blk_rows), blk_rows), :],
                              gsem.at[buf_slot]).wait()

    def wait_scatter():
        pltpu.make_async_copy(ybuf, y_hbm.at[pl.ds(0, blk_rows), :], ssem.at[0]).wait()

    def issue_step_copies():
        for next_slot in range(2):
            for step in range(nf):
                @pl.when(jnp.logical_and(slot == 1 - next_slot, f == step))
                def _():
                    for r in range(step * per_step, (step + 1) * per_step):
                        gather_copy(src_nxt, next_slot, r).start()
                        scatter_copy(dst_prv, r).start()

    @pl.when(b < n_used)
    def _():
        @pl.when(f == 0)
        def _():
            @pl.when(b == 0)
            def _():
                def body(r, carry):
                    gather_copy(src_cur, 0, r).start()
                    return carry
                lax.fori_loop(0, bm, body, 0, unroll=DMA_UNROLL)
                ybuf[...] = jnp.zeros_like(ybuf)

            wait_gather(slot)
            base = slot * blk_rows
            for s in range(rt):
                xs_ref[:, s * LANES:(s + 1) * LANES] = (
                    xbuf[pl.ds(base + s, bm, stride=rt), :].astype(BF16))
            acc_ref[...] = jnp.zeros_like(acc_ref)

        xb = xs_ref[...]
        act = (_silu(_dot(xb, w13_ref[:, :tf])) * _dot(xb, w13_ref[:, tf:])).astype(BF16)
        acc_ref[...] += _dot(act, w2_ref[...])
        issue_step_copies()

        @pl.when(f == nf - 1)
        def _():
            wait_scatter()
            for s in range(rt):
                ybuf[pl.ds(s, bm, stride=rt), :] = acc_ref[:, s * LANES:(s + 1) * LANES]

            @pl.when(b == n_used - 1)
            def _():
                def body(r, carry):
                    scatter_copy(dst_cur, r).start()
                    return carry
                lax.fori_loop(0, bm, body, 0, unroll=DMA_UNROLL)
                wait_scatter()
                wait_gather(1 - slot)

    @pl.when(jnp.logical_and(b >= n_used, f == pl.num_programs(1) - 1))
    def _():
        ybuf[...] = jnp.zeros_like(ybuf)
        fill = pltpu.make_async_copy(
            ybuf, y_hbm.at[pl.ds(pl.multiple_of(b * blk_rows, blk_rows), blk_rows), :], ssem.at[0])
        fill.start()
        fill.wait()


def _moe_experts(h3, slot_src, slot_dst, block_expert, n_used, n_blocks, w1, w3, w2):
    D = D_MODEL
    bm, tf = BM_MOE, TF_MOE
    E, _, F = w1.shape
    nf = F // tf
    n_slots = n_blocks * bm

    w13 = jnp.stack([w1.reshape(E, D, nf, tf), w3.reshape(E, D, nf, tf)], axis=3)
    w13 = w13.reshape(E, D, nf * 2 * tf)
    nxt = jnp.minimum(jnp.arange(n_blocks) + 1, n_blocks - 1)
    prv = jnp.maximum(jnp.arange(n_blocks) - 1, 0)
    idx = jnp.concatenate([slot_src, slot_dst, slot_src[nxt], slot_dst[prv]], axis=-1)

    def fsel(b, f, nu):
        return jnp.where(b < nu[0], f, nf - 1)

    grid_spec = pltpu.PrefetchScalarGridSpec(
        num_scalar_prefetch=2,
        grid=(n_blocks, nf),
        in_specs=[
            pl.BlockSpec((1, 1, 4 * bm), lambda b, f, be, nu: (b, 0, 0), memory_space=pltpu.SMEM),
            pl.BlockSpec(memory_space=pl.ANY),
            pl.BlockSpec((None, D, 2 * tf), lambda b, f, be, nu: (be[b], 0, fsel(b, f, nu))),
            pl.BlockSpec((None, tf, D), lambda b, f, be, nu: (be[b], fsel(b, f, nu), 0)),
        ],
        out_specs=pl.BlockSpec(memory_space=pl.ANY),
        scratch_shapes=[
            pltpu.VMEM((2 * bm * ROW_TILES, LANES), F32),
            pltpu.VMEM((bm, D), BF16),
            pltpu.VMEM((bm, D), F32),
            pltpu.VMEM((bm * ROW_TILES, LANES), F32),
            pltpu.SemaphoreType.DMA((2,)),
            pltpu.SemaphoreType.DMA((1,)),
        ],
    )
    return pl.pallas_call(
        functools.partial(_moe_kernel, nf),
        grid_spec=grid_spec,
        out_shape=jax.ShapeDtypeStruct((n_slots * ROW_TILES, LANES), F32),
        compiler_params=_params("arbitrary", "arbitrary"),
        name="moe_experts",
    )(block_expert, n_used, idx, h3, w13, w2)


def _combine_kernel(x_ref, mod_ref, y0_ref, y1_ref, gate_ref, fg_ref, o_ref, f_ref):
    tm = x_ref.shape[0]
    g = gate_ref[...]
    for s in range(ROW_TILES):
        rows = pl.ds(s, tm, stride=ROW_TILES)
        f_ref[:, s * LANES:(s + 1) * LANES] = g[:, 0:1] * y0_ref[rows, :] + g[:, 1:2] * y1_ref[rows, :]
    x = x_ref[...] + mod_ref[0, 5:6, :] * f_ref[...]
    o_ref[...] = (x * lax.rsqrt(jnp.mean(x * x, axis=-1, keepdims=True) + EPS)) * fg_ref[...]


def _combine(x2, mod, y_slots, gates, final_g, seq):
    T, D = x2.shape
    tm = TM_OUT
    tiles_per_seq = seq // tm
    n_tiles = T // tm
    row = pl.BlockSpec((tm, D), lambda i: (i, 0))
    return pl.pallas_call(
        _combine_kernel,
        grid=(T // tm,),
        in_specs=[
            row,
            pl.BlockSpec((1, N_MOD, D), lambda i: (i // tiles_per_seq, 0, 0)),
            pl.BlockSpec((tm * ROW_TILES, LANES), lambda i: (i, 0)),
            pl.BlockSpec((tm * ROW_TILES, LANES), lambda i: (n_tiles + i, 0)),
            pl.BlockSpec((tm, TOP_K), lambda i: (i, 0)),
            _resident((1, D)),
        ],
        out_specs=row,
        out_shape=jax.ShapeDtypeStruct((T, D), F32),
        scratch_shapes=[pltpu.VMEM((tm, D), F32)],
        compiler_params=_params("arbitrary"),
        name="combine_norm",
    )(x2, mod, y_slots, y_slots, gates, final_g.reshape(1, D))


def _mixer_layer(x2, mod, i, batch, seq, p):
    parts = _mixer_in(x2, mod, p["norm1_g"][i], p["w_in"][i].astype(BF16), p["b_in"][i],
                      p["sgu_ln_g"][i], p["sgu_ln_b"][i], seq)
    return _mixer_mid(x2, mod, parts, p["conv_w"][i], p["conv_b"][i], p["conv_ln_g"][i],
                      p["conv_ln_b"][i], p["conv_pw_w"][i].astype(BF16), p["conv_pw_b"][i],
                      p["pool_w"][i].astype(BF16), p["pool_b"][i], p["pool_scale"][i],
                      p["sgu_w"][i], p["sgu_b"][i], p["w_out"][i].astype(BF16), batch, seq)


def kernel(x, c, ada_w, ada_b, norm1_g, norm2_g, w_in, b_in, conv_w, conv_b, conv_ln_g, conv_ln_b, conv_pw_w, conv_pw_b, pool_w, pool_b, pool_scale, sgu_ln_g, sgu_ln_b, sgu_w, sgu_b, w_out, ffn_w1, ffn_w3, ffn_w2, router_w, moe_w1, moe_w3, moe_w2, final_norm_g):
    p = dict(norm1_g=norm1_g, w_in=w_in, b_in=b_in, conv_w=conv_w, conv_b=conv_b,
             conv_ln_g=conv_ln_g, conv_ln_b=conv_ln_b, conv_pw_w=conv_pw_w, conv_pw_b=conv_pw_b,
             pool_w=pool_w, pool_b=pool_b, pool_scale=pool_scale, sgu_ln_g=sgu_ln_g,
             sgu_ln_b=sgu_ln_b, sgu_w=sgu_w, sgu_b=sgu_b, w_out=w_out)
    batch, seq, D = x.shape
    T = batch * seq
    mod = _ada_mod(c, ada_w, ada_b)
    x2 = x.reshape(T, D)

    x2 = _mixer_layer(x2, mod[0], 0, batch, seq, p)
    x2 = _ffn(x2, mod[0], norm2_g[0], ffn_w1[0].astype(BF16), ffn_w3[0].astype(BF16),
              ffn_w2[0].astype(BF16), seq)

    x2 = _mixer_layer(x2, mod[1], 1, batch, seq, p)
    h3, top_idx, gates = _router(x2, mod[1], norm2_g[1], router_w[0], seq)
    slot_src, slot_dst, block_expert, n_used, n_blocks = _dispatch_plan(top_idx, T)
    y_slots = _moe_experts(h3, slot_src, slot_dst, block_expert, n_used, n_blocks,
                           moe_w1[0].astype(BF16), moe_w3[0].astype(BF16), moe_w2[0].astype(BF16))
    out = _combine(x2, mod[1], y_slots, gates, final_norm_g, seq)
    return out.reshape(batch, seq, D)
```

```python
import functools
import math

import jax
import jax.numpy as jnp
from jax import lax
from jax.experimental import pallas as pl
from jax.experimental.pallas import tpu as pltpu

F32 = jnp.float32
BF16 = jnp.bfloat16

D_MODEL = 1024
CONV_WIDTH = 31
POOL_WINDOWS = (2, 4, 8, 16)
POOL_GROUP = D_MODEL // len(POOL_WINDOWS)
SGU_CHUNK = 128
N_SGU_HEADS = 8
N_EXPERTS = 8
TOP_K = 2
N_MOD = 6
EPS = 1e-6

LANES = 128
SUBLANES = 8
ROW_TILES = D_MODEL // LANES

CONV_HALO = 32
POOL_HALO = 32
CONV_ROWS = 128

VMEM_LIMIT = 56 * 1024 * 1024

TM_IN = 512
TS_MIX = 512
TM_FFN = 512
TM_ROUTE = 512
BM_MOE = 512
TF_MOE = 1792
TM_OUT = 512
DMA_UNROLL = 8


def _sigmoid(x):
    return 0.5 * (1.0 + jnp.tanh(0.5 * x))


def _silu(x):
    return x * _sigmoid(x)


def _gelu_tanh(x):
    c = math.sqrt(2.0 / math.pi)
    return x * (0.5 * (1.0 + jnp.tanh(c * (x + 0.044715 * (x * x * x)))))


def _dot(a, b):
    return jnp.dot(a, b, preferred_element_type=F32)


def _layernorm(x, g, b):
    mu = jnp.mean(x, axis=-1, keepdims=True)
    xc = x - mu
    var = jnp.mean(xc * xc, axis=-1, keepdims=True)
    return xc * lax.rsqrt(var + EPS) * g + b


def _ada_rmsnorm(x, g, scale, shift):
    y = x * lax.rsqrt(jnp.mean(x * x, axis=-1, keepdims=True) + EPS)
    return (y * g) * (1.0 + scale) + shift


def _params(*sem):
    return pltpu.CompilerParams(dimension_semantics=sem, vmem_limit_bytes=VMEM_LIMIT)


def _resident(shape):
    nd = len(shape)
    return pl.BlockSpec(shape, lambda *_: (0,) * nd, pipeline_mode=pl.Buffered(1))


def _ada_kernel(c_ref, w_ref, b_ref, o_ref):
    ca = _silu(c_ref[...]).astype(BF16)
    o_ref[...] = _dot(ca, w_ref[...].astype(BF16)) + b_ref[...]


def _ada_mod(c, ada_w, ada_b):
    L, D, N = ada_w.shape
    B = c.shape[0]
    out = pl.pallas_call(
        _ada_kernel,
        grid=(L, N // D),
        in_specs=[
            pl.BlockSpec((B, D), lambda l, j: (0, 0)),
            pl.BlockSpec((None, D, D), lambda l, j: (l, 0, j)),
            pl.BlockSpec((None, 1, D), lambda l, j: (l, 0, j)),
        ],
        out_specs=pl.BlockSpec((None, B, D), lambda l, j: (l, 0, j)),
        out_shape=jax.ShapeDtypeStruct((L, B, N), F32),
        compiler_params=_params("arbitrary", "arbitrary"),
        name="ada_mod",
    )(c, ada_w, ada_b.reshape(L, 1, N))
    return out.reshape(L, B, N_MOD, D)


def _mixer_in_kernel(x_ref, mod_ref, ng_ref, w_ref, b_ref, lng_ref, lnb_ref,
                     ua_ref, p_ref, su_ref, sv_ref, g0_ref, g1_ref, g2_ref):
    D = D_MODEL
    h = _ada_rmsnorm(x_ref[...], ng_ref[...], mod_ref[0, 1:2, :], mod_ref[0, 0:1, :])
    hb = h.astype(BF16)

    def proj(j):
        return _dot(hb, w_ref[:, j * D:(j + 1) * D]) + b_ref[:, j * D:(j + 1) * D]

    ua_ref[...] = (proj(0) * _sigmoid(proj(1))).astype(BF16)
    p_ref[...] = proj(2).astype(BF16)
    su_ref[...] = _gelu_tanh(proj(3)).astype(BF16)
    sv_ref[...] = _layernorm(_gelu_tanh(proj(4)), lng_ref[...], lnb_ref[...]).astype(BF16)
    g0_ref[...] = _sigmoid(proj(5)).astype(BF16)
    g1_ref[...] = _sigmoid(proj(6)).astype(BF16)
    g2_ref[...] = _sigmoid(proj(7)).astype(BF16)


def _mixer_in(x2, mod, norm_g, w_in, b_in, sgu_ln_g, sgu_ln_b, seq):
    T, D = x2.shape
    tm = TM_IN
    tiles_per_seq = seq // tm
    row = pl.BlockSpec((tm, D), lambda i: (i, 0))
    outs = pl.pallas_call(
        _mixer_in_kernel,
        grid=(T // tm,),
        in_specs=[
            row,
            pl.BlockSpec((1, N_MOD, D), lambda i: (i // tiles_per_seq, 0, 0)),
            _resident((1, D)),
            _resident(w_in.shape),
            _resident((1, w_in.shape[1])),
            _resident((1, D)),
            _resident((1, D)),
        ],
        out_specs=[row] * 7,
        out_shape=[jax.ShapeDtypeStruct((T, D), BF16)] * 7,
        compiler_params=_params("arbitrary"),
        name="mixer_in",
    )(x2, mod, norm_g.reshape(1, D), w_in, b_in.reshape(1, -1),
      sgu_ln_g.reshape(1, D), sgu_ln_b.reshape(1, D))
    return outs


def _mixer_mid_kernel(x_ref, mod_ref, ua_ref, p_ref, su_ref, sv_ref, g0_ref, g1_ref, g2_ref,
                      cw_ref, cb_ref, clg_ref, clb_ref, pw_ref, pb_ref,
                      plw_ref, plb_ref, pls_ref, sw_ref, sbt_ref, wo_ref,
                      o_ref, u_ext, p_ext, cv_ref, mg_ref, q_ext):
    D = D_MODEL
    ts = TS_MIX
    s_idx = pl.program_id(1)

    @pl.when(s_idx == 0)
    def _():
        u_ext[0:CONV_HALO, :] = jnp.zeros((CONV_HALO, D), F32)
        p_ext[0:POOL_HALO, :] = jnp.zeros((POOL_HALO, D), F32)

    @pl.when(s_idx > 0)
    def _():
        u_ext[0:CONV_HALO, :] = u_ext[ts:ts + CONV_HALO, :]
        p_ext[0:POOL_HALO, :] = p_ext[ts:ts + POOL_HALO, :]

    u_ext[CONV_HALO:CONV_HALO + ts, :] = ua_ref[...].astype(F32)
    p_ext[POOL_HALO:POOL_HALO + ts, :] = p_ref[...].astype(F32)

    ri = lax.broadcasted_iota(jnp.int32, (SGU_CHUNK, SGU_CHUNK), 0)
    ci = lax.broadcasted_iota(jnp.int32, (SGU_CHUNK, SGU_CHUNK), 1)
    causal = ri >= ci
    for hd in range(N_SGU_HEADS):
        cols = slice(hd * SGU_CHUNK, (hd + 1) * SGU_CHUNK)
        wm = jnp.where(causal, sw_ref[hd], 0.0).astype(BF16)
        bcol = sbt_ref[:, hd:hd + 1]
        for n in range(ts // SGU_CHUNK):
            rows = slice(n * SGU_CHUNK, (n + 1) * SGU_CHUNK)
            mixed = _dot(wm, sv_ref[rows, cols]) + bcol
            mg_ref[rows, cols] = (g2_ref[rows, cols].astype(F32)
                                  * su_ref[rows, cols].astype(F32) * mixed)

    pos1 = (s_idx * ts + lax.broadcasted_iota(jnp.int32, (ts, 1), 0) + 1).astype(F32)
    for gi, w in enumerate(POOL_WINDOWS):
        cols = slice(gi * POOL_GROUP, (gi + 1) * POOL_GROUP)
        cur = p_ext[POOL_HALO:POOL_HALO + ts, cols]
        read = lambda lo, n, cols=cols: p_ext[lo:lo + n, cols]
        k, lo = 1, SUBLANES
        while True:
            n = ts + POOL_HALO - lo
            nxt = read(lo, n) + read(lo - k, n)
            k *= 2
            if k == w:
                win = nxt[POOL_HALO - lo:, :]
                break
            lvl = q_ext.at[(lo // SUBLANES) % 2]
            lvl[lo:lo + n, :] = nxt
            read = lambda lo, n, lvl=lvl: lvl[lo:lo + n, :]
            lo += SUBLANES
        d = win * (1.0 / jnp.minimum(pos1, float(w))) - cur
        yb = (_dot(d.astype(BF16), plw_ref[gi]) + plb_ref[:, cols]) * pls_ref[:, cols]
        mg_ref[:, cols] += g1_ref[:, cols].astype(F32) * yb

    rc = CONV_ROWS
    n_a = -(-CONV_WIDTH // SUBLANES)

    def conv_rows(ci, carry):
        r0 = pl.multiple_of(ci * rc, rc)
        for ct in range(D // LANES):
            cols = slice(ct * LANES, (ct + 1) * LANES)
            acc = jnp.zeros((rc, LANES), F32) + cb_ref[:, cols]
            for r in range(SUBLANES):
                v = None
                for a in range(n_a):
                    j = SUBLANES * a + r
                    if j >= CONV_WIDTH:
                        continue
                    start = r0 + (CONV_HALO - SUBLANES - SUBLANES * a)
                    term = (u_ext[pl.ds(start, rc + SUBLANES), cols]
                            * cw_ref[CONV_WIDTH - 1 - j:CONV_WIDTH - j, cols])
                    v = term if v is None else v + term
                acc = acc + v[SUBLANES - r:SUBLANES - r + rc, :]
            cv_ref[pl.ds(r0, rc), cols] = acc
        return carry

    lax.fori_loop(0, ts // rc, conv_rows, 0)
    cn = _silu(_layernorm(cv_ref[...], clg_ref[...], clb_ref[...])).astype(BF16)
    ya = _dot(cn, pw_ref[...]) + pb_ref[...]
    merged = mg_ref[...] + g0_ref[...].astype(F32) * ya

    y = _dot(merged.astype(BF16), wo_ref[...])
    o_ref[...] = x_ref[...] + mod_ref[0, 2:3, :] * y


def _mixer_mid(x2, mod, parts, conv_w, conv_b, conv_ln_g, conv_ln_b, conv_pw_w, conv_pw_b,
               pool_w, pool_b, pool_scale, sgu_w, sgu_b, w_out, batch, seq):
    T, D = x2.shape
    ts = TS_MIX
    ns = seq // ts
    row = pl.BlockSpec((ts, D), lambda b, s: (b * ns + s, 0))
    vec = lambda a: a.reshape(1, D)
    return pl.pallas_call(
        _mixer_mid_kernel,
        grid=(batch, ns),
        in_specs=[
            row,
            pl.BlockSpec((1, N_MOD, D), lambda b, s: (b, 0, 0)),
        ] + [row] * 7 + [
            _resident((CONV_WIDTH, D)), _resident((1, D)), _resident((1, D)), _resident((1, D)),
            _resident((D, D)), _resident((1, D)),
            _resident(pool_w.shape), _resident((1, D)), _resident((1, D)),
            _resident(sgu_w.shape), _resident((SGU_CHUNK, N_SGU_HEADS)),
            _resident((D, D)),
        ],
        out_specs=row,
        out_shape=jax.ShapeDtypeStruct((T, D), F32),
        scratch_shapes=[
            pltpu.VMEM((ts + CONV_HALO, D), F32),
            pltpu.VMEM((ts + POOL_HALO, D), F32),
            pltpu.VMEM((ts, D), F32),
            pltpu.VMEM((ts, D), F32),
            pltpu.VMEM((2, ts + POOL_HALO, POOL_GROUP), F32),
        ],
        compiler_params=_params("arbitrary", "arbitrary"),
        name="mixer_mid",
    )(x2, mod, *parts, conv_w, vec(conv_b), vec(conv_ln_g), vec(conv_ln_b), conv_pw_w,
      vec(conv_pw_b), pool_w, vec(pool_b), vec(pool_scale), sgu_w, sgu_b.T, w_out)


def _ffn_kernel(x_ref, mod_ref, ng_ref, w1_ref, w3_ref, w2_ref, o_ref):
    x = x_ref[...]
    hb = _ada_rmsnorm(x, ng_ref[...], mod_ref[0, 4:5, :], mod_ref[0, 3:4, :]).astype(BF16)
    act = (_silu(_dot(hb, w1_ref[...])) * _dot(hb, w3_ref[...])).astype(BF16)
    o_ref[...] = x + mod_ref[0, 5:6, :] * _dot(act, w2_ref[...])


def _ffn(x2, mod, norm_g, w1, w3, w2, seq):
    T, D = x2.shape
    tm = TM_FFN
    tiles_per_seq = seq // tm
    row = pl.BlockSpec((tm, D), lambda i: (i, 0))
    return pl.pallas_call(
        _ffn_kernel,
        grid=(T // tm,),
        in_specs=[
            row,
            pl.BlockSpec((1, N_MOD, D), lambda i: (i // tiles_per_seq, 0, 0)),
            _resident((1, D)),
            _resident(w1.shape),
            _resident(w3.shape),
            _resident(w2.shape),
        ],
        out_specs=row,
        out_shape=jax.ShapeDtypeStruct((T, D), F32),
        compiler_params=_params("arbitrary"),
        name="ffn",
    )(x2, mod, norm_g.reshape(1, D), w1, w3, w2)


def _split_bf16(v):
    hi = v.astype(BF16)
    lo = (v - hi.astype(F32)).astype(BF16)
    return hi, lo


def _router_kernel(x_ref, mod_ref, ng_ref, rw_ref, h_ref, idx_ref, gate_ref):
    tm = x_ref.shape[0]
    h = _ada_rmsnorm(x_ref[...], ng_ref[...], mod_ref[0, 4:5, :], mod_ref[0, 3:4, :])
    for s in range(ROW_TILES):
        h_ref[pl.ds(s, tm, stride=ROW_TILES), :] = h[:, s * LANES:(s + 1) * LANES]

    h_hi, h_lo = _split_bf16(h)
    w_hi, w_lo = _split_bf16(rw_ref[...])
    logits = _dot(h_hi, w_hi) + (_dot(h_lo, w_hi) + _dot(h_hi, w_lo))

    lane = lax.broadcasted_iota(jnp.int32, (tm, LANES), 1)
    neg = jnp.float32(-jnp.inf)
    logits = jnp.where(lane < N_EXPERTS, logits, neg)
    m1 = jnp.max(logits, axis=-1, keepdims=True)
    i1 = jnp.min(jnp.where(logits == m1, lane, LANES), axis=-1, keepdims=True)
    rest = jnp.where(lane == i1, neg, logits)
    m2 = jnp.max(rest, axis=-1, keepdims=True)
    i2 = jnp.min(jnp.where(rest == m2, lane, LANES), axis=-1, keepdims=True)
    e2 = jnp.exp(m2 - m1)
    den = 1.0 + e2
    idx_ref[...] = jnp.where(lane == 0, i1, i2)[:, :TOP_K]
    gate_ref[...] = jnp.where(lane == 0, 1.0 / den, e2 / den)[:, :TOP_K]


def _router(x2, mod, norm_g, router_w, seq):
    T, D = x2.shape
    tm = TM_ROUTE
    tiles_per_seq = seq // tm
    rw = jnp.zeros((D, LANES), F32).at[:, :N_EXPERTS].set(router_w)
    return pl.pallas_call(
        _router_kernel,
        grid=(T // tm,),
        in_specs=[
            pl.BlockSpec((tm, D), lambda i: (i, 0)),
            pl.BlockSpec((1, N_MOD, D), lambda i: (i // tiles_per_seq, 0, 0)),
            _resident((1, D)),
            _resident((D, LANES)),
        ],
        out_specs=[
            pl.BlockSpec((tm * ROW_TILES, LANES), lambda i: (i, 0)),
            pl.BlockSpec((tm, TOP_K), lambda i: (i, 0)),
            pl.BlockSpec((tm, TOP_K), lambda i: (i, 0)),
        ],
        out_shape=[
            jax.ShapeDtypeStruct((T * ROW_TILES, LANES), F32),
            jax.ShapeDtypeStruct((T, TOP_K), jnp.int32),
            jax.ShapeDtypeStruct((T, TOP_K), F32),
        ],
        compiler_params=_params("arbitrary"),
        name="router",
    )(x2, mod, norm_g.reshape(1, D), rw)


def _dispatch_plan(top_idx, n_tokens):
    bm = BM_MOE
    n_flat = n_tokens * TOP_K
    n_blocks = -(-(n_flat + N_EXPERTS * (bm - 1)) // bm)
    n_slots = n_blocks * bm
    flat_e = top_idx.reshape(-1)
    order = jnp.argsort(flat_e, stable=True).astype(jnp.int32)
    counts = jnp.sum(flat_e[:, None] == jnp.arange(N_EXPERTS, dtype=jnp.int32)[None, :],
                     axis=0, dtype=jnp.int32)
    padded = (counts + bm - 1) // bm * bm
    start_sorted = jnp.cumsum(counts) - counts
    ends_padded = jnp.cumsum(padded)
    start_padded = ends_padded - padded
    n_used = (ends_padded[-1] // bm).astype(jnp.int32)
    block_start = jnp.arange(n_blocks, dtype=jnp.int32) * bm
    block_expert = jnp.minimum(jnp.searchsorted(ends_padded, block_start, side="right"),
                               N_EXPERTS - 1).astype(jnp.int32)
    last_e = block_expert[jnp.maximum(n_used - 1, 0)]
    block_expert = jnp.where(jnp.arange(n_blocks) < n_used, block_expert, last_e)
    slot = jnp.arange(n_slots, dtype=jnp.int32)
    slot_e = jnp.repeat(block_expert, bm)
    rank = slot - start_padded[slot_e]
    valid = (rank < counts[slot_e]) & (slot < ends_padded[-1])
    flat = order[jnp.clip(start_sorted[slot_e] + rank, 0, n_flat - 1)]
    spare = n_flat + jnp.cumsum(jnp.logical_not(valid).astype(jnp.int32)) - 1
    slot_src = jnp.where(valid, flat // TOP_K, 0).astype(jnp.int32)
    slot_dst = jnp.where(valid, (flat % TOP_K) * n_tokens + flat // TOP_K, spare).astype(jnp.int32)
    return (slot_src.reshape(n_blocks, 1, bm), slot_dst.reshape(n_blocks, 1, bm),
            block_expert, n_used.reshape(1), n_blocks)


def _moe_kernel(nf, be_ref, nu_ref, src_cur, src_nxt, dst_cur, dst_prv, h_hbm, w1_ref, w3_ref,
                w2_ref, y_hbm, xbuf, xs_ref, acc_ref, ybuf, gsem, ssem):
    bm = BM_MOE
    b = pl.program_id(0)
    f = pl.program_id(1)
    n_used = nu_ref[0]
    slot = b % 2

    rt = ROW_TILES
    blk_rows = bm * rt
    per_step = bm // nf

    def token_rows(ref, tok):
        return ref.at[pl.ds(pl.multiple_of(tok * rt, rt), rt), :]

    def gather_copy(idx_ref, buf_slot, r):
        return pltpu.make_async_copy(token_rows(h_hbm, idx_ref[0, 0, r]),
                                     token_rows(xbuf, buf_slot * bm + r), gsem.at[buf_slot])

    def scatter_copy(idx_ref, r):
        return pltpu.make_async_copy(token_rows(ybuf, r), token_rows(y_hbm, idx_ref[0, 0, r]),
                                     ssem.at[0])

    def wait_gather(buf_slot):
        pltpu.make_async_copy(h_hbm.at[pl.ds(0, blk_rows), :],
                              xbuf.at[pl.ds(pl.multiple_of(buf_slot * blk_rows, blk_rows), blk_rows), :],
                              gsem.at[buf_slot]).wait()

    def wait_scatter():
        pltpu.make_async_copy(ybuf, y_hbm.at[pl.ds(0, blk_rows), :], ssem.at[0]).wait()

    def issue_piece(piece, n_pieces):
        lo = per_step * piece // n_pieces
        hi = per_step * (piece + 1) // n_pieces
        for k in range(lo, hi):
            r = f * per_step + k
            gather_copy(src_nxt, 1 - slot, r).start()
            scatter_copy(dst_prv, r).start(priority=k % 2)

    @pl.when(b < n_used)
    def _():
        @pl.when(f == 0)
        def _():
            @pl.when(b == 0)
            def _():
                def body(r, carry):
                    gather_copy(src_cur, 0, r).start()
                    return carry
                lax.fori_loop(0, bm, body, 0, unroll=DMA_UNROLL)
                ybuf[...] = jnp.zeros_like(ybuf)

            wait_gather(slot)
            base = slot * blk_rows
            for s in range(rt):
                xs_ref[:, s * LANES:(s + 1) * LANES] = (
                    xbuf[pl.ds(base + s, bm, stride=rt), :].astype(BF16))
            acc_ref[...] = jnp.zeros_like(acc_ref)

        xb = xs_ref[...]
        issue_piece(0, 3)
        h1 = _dot(xb, w1_ref[...])
        issue_piece(1, 3)
        h3 = _dot(xb, w3_ref[...])
        issue_piece(2, 3)
        act = (_silu(h1) * h3).astype(BF16)
        acc_ref[...] += _dot(act, w2_ref[...])

        @pl.when(f == nf - 1)
        def _():
            wait_scatter()
            for s in range(rt):
                ybuf[pl.ds(s, bm, stride=rt), :] = acc_ref[:, s * LANES:(s + 1) * LANES]

            @pl.when(b == n_used - 1)
            def _():
                def body(r, carry):
                    scatter_copy(dst_cur, r).start()
                    return carry
                lax.fori_loop(0, bm, body, 0, unroll=DMA_UNROLL)
                wait_scatter()
                wait_gather(1 - slot)

    @pl.when(jnp.logical_and(b >= n_used, f == pl.num_programs(1) - 1))
    def _():
        ybuf[...] = jnp.zeros_like(ybuf)
        fill = pltpu.make_async_copy(
            ybuf, y_hbm.at[pl.ds(pl.multiple_of(b * blk_rows, blk_rows), blk_rows), :], ssem.at[0])
        fill.start()
        fill.wait()


def _moe_experts(h3, slot_src, slot_dst, block_expert, n_used, n_blocks, w1, w3, w2):
    D = D_MODEL
    bm, tf = BM_MOE, TF_MOE
    F = w1.shape[2]
    nf = F // tf
    n_slots = n_blocks * bm

    def fsel(b, f, nu):
        return jnp.where(b < nu[0], f, nf - 1)

    smem_blk = lambda imap: pl.BlockSpec((1, 1, bm), imap, memory_space=pltpu.SMEM)
    grid_spec = pltpu.PrefetchScalarGridSpec(
        num_scalar_prefetch=2,
        grid=(n_blocks, nf),
        in_specs=[
            smem_blk(lambda b, f, be, nu: (b, 0, 0)),
            smem_blk(lambda b, f, be, nu: (jnp.minimum(b + 1, n_blocks - 1), 0, 0)),
            smem_blk(lambda b, f, be, nu: (b, 0, 0)),
            smem_blk(lambda b, f, be, nu: (jnp.maximum(b - 1, 0), 0, 0)),
            pl.BlockSpec(memory_space=pl.ANY),
            pl.BlockSpec((None, D, tf), lambda b, f, be, nu: (be[b], 0, fsel(b, f, nu))),
            pl.BlockSpec((None, D, tf), lambda b, f, be, nu: (be[b], 0, fsel(b, f, nu))),
            pl.BlockSpec((None, tf, D), lambda b, f, be, nu: (be[b], fsel(b, f, nu), 0)),
        ],
        out_specs=pl.BlockSpec(memory_space=pl.ANY),
        scratch_shapes=[
            pltpu.VMEM((2 * bm * ROW_TILES, LANES), F32),
            pltpu.VMEM((bm, D), BF16),
            pltpu.VMEM((bm, D), F32),
            pltpu.VMEM((bm * ROW_TILES, LANES), F32),
            pltpu.SemaphoreType.DMA((2,)),
            pltpu.SemaphoreType.DMA((1,)),
        ],
    )
    return pl.pallas_call(
        functools.partial(_moe_kernel, nf),
        grid_spec=grid_spec,
        out_shape=jax.ShapeDtypeStruct((n_slots * ROW_TILES, LANES), F32),
        compiler_params=_params("arbitrary", "arbitrary"),
        name="moe_experts",
    )(block_expert, n_used, slot_src, slot_src, slot_dst, slot_dst, h3, w1, w3, w2)


def _combine_kernel(x_ref, mod_ref, y0_ref, y1_ref, gate_ref, fg_ref, o_ref, f_ref):
    tm = x_ref.shape[0]
    g = gate_ref[...]
    for s in range(ROW_TILES):
        rows = pl.ds(s, tm, stride=ROW_TILES)
        f_ref[:, s * LANES:(s + 1) * LANES] = g[:, 0:1] * y0_ref[rows, :] + g[:, 1:2] * y1_ref[rows, :]
    x = x_ref[...] + mod_ref[0, 5:6, :] * f_ref[...]
    o_ref[...] = (x * lax.rsqrt(jnp.mean(x * x, axis=-1, keepdims=True) + EPS)) * fg_ref[...]


def _combine(x2, mod, y_slots, gates, final_g, seq):
    T, D = x2.shape
    tm = TM_OUT
    tiles_per_seq = seq // tm
    n_tiles = T // tm
    row = pl.BlockSpec((tm, D), lambda i: (i, 0))
    return pl.pallas_call(
        _combine_kernel,
        grid=(T // tm,),
        in_specs=[
            row,
            pl.BlockSpec((1, N_MOD, D), lambda i: (i // tiles_per_seq, 0, 0)),
            pl.BlockSpec((tm * ROW_TILES, LANES), lambda i: (i, 0)),
            pl.BlockSpec((tm * ROW_TILES, LANES), lambda i: (n_tiles + i, 0)),
            pl.BlockSpec((tm, TOP_K), lambda i: (i, 0)),
            _resident((1, D)),
        ],
        out_specs=row,
        out_shape=jax.ShapeDtypeStruct((T, D), F32),
        scratch_shapes=[pltpu.VMEM((tm, D), F32)],
        compiler_params=_params("arbitrary"),
        name="combine_norm",
    )(x2, mod, y_slots, y_slots, gates, final_g.reshape(1, D))


def _mixer_layer(x2, mod, i, batch, seq, p):
    parts = _mixer_in(x2, mod, p["norm1_g"][i], p["w_in"][i].astype(BF16), p["b_in"][i],
                      p["sgu_ln_g"][i], p["sgu_ln_b"][i], seq)
    return _mixer_mid(x2, mod, parts, p["conv_w"][i], p["conv_b"][i], p["conv_ln_g"][i],
                      p["conv_ln_b"][i], p["conv_pw_w"][i].astype(BF16), p["conv_pw_b"][i],
                      p["pool_w"][i].astype(BF16), p["pool_b"][i], p["pool_scale"][i],
                      p["sgu_w"][i], p["sgu_b"][i], p["w_out"][i].astype(BF16), batch, seq)


def kernel(x, c, ada_w, ada_b, norm1_g, norm2_g, w_in, b_in, conv_w, conv_b, conv_ln_g, conv_ln_b, conv_pw_w, conv_pw_b, pool_w, pool_b, pool_scale, sgu_ln_g, sgu_ln_b, sgu_w, sgu_b, w_out, ffn_w1, ffn_w3, ffn_w2, router_w, moe_w1, moe_w3, moe_w2, final_norm_g):
    p = dict(norm1_g=norm1_g, w_in=w_in, b_in=b_in, conv_w=conv_w, conv_b=conv_b,
             conv_ln_g=conv_ln_g, conv_ln_b=conv_ln_b, conv_pw_w=conv_pw_w, conv_pw_b=conv_pw_b,
             pool_w=pool_w, pool_b=pool_b, pool_scale=pool_scale, sgu_ln_g=sgu_ln_g,
             sgu_ln_b=sgu_ln_b, sgu_w=sgu_w, sgu_b=sgu_b, w_out=w_out)
    batch, seq, D = x.shape
    T = batch * seq
    mod = _ada_mod(c, ada_w, ada_b)
    x2 = x.reshape(T, D)

    x2 = _mixer_layer(x2, mod[0], 0, batch, seq, p)
    x2 = _ffn(x2, mod[0], norm2_g[0], ffn_w1[0].astype(BF16), ffn_w3[0].astype(BF16),
              ffn_w2[0].astype(BF16), seq)

    x2 = _mixer_layer(x2, mod[1], 1, batch, seq, p)
    h3, top_idx, gates = _router(x2, mod[1], norm2_g[1], router_w[0], seq)
    slot_src, slot_dst, block_expert, n_used, n_blocks = _dispatch_plan(top_idx, T)
    y_slots = _moe_experts(h3, slot_src, slot_dst, block_expert, n_used, n_blocks,
                           moe_w1[0].astype(BF16), moe_w3[0].astype(BF16), moe_w2[0].astype(BF16))
    out = _combine(x2, mod[1], y_slots, gates, final_norm_g, seq)
    return out.reshape(batch, seq, D)
```

```python
import functools
import math

import jax
import jax.numpy as jnp
from jax import lax
from jax.experimental import pallas as pl
from jax.experimental.pallas import tpu as pltpu

F32 = jnp.float32
BF16 = jnp.bfloat16

D_MODEL = 1024
CONV_WIDTH = 31
POOL_WINDOWS = (2, 4, 8, 16)
POOL_GROUP = D_MODEL // len(POOL_WINDOWS)
SGU_CHUNK = 128
N_SGU_HEADS = 8
N_EXPERTS = 8
TOP_K = 2
N_MOD = 6
EPS = 1e-6

LANES = 128
SUBLANES = 8
ROW_TILES = D_MODEL // LANES

CONV_HALO = 32
POOL_HALO = 32
CONV_ROWS = 128

VMEM_LIMIT = 56 * 1024 * 1024

TM_IN = 512
TS_MIX = 512
TM_FFN = 512
TM_ROUTE = 1024
BM_MOE = 512
TF_MOE = 1792
TM_OUT = 1024
DMA_UNROLL = 8


def _sigmoid(x):
    return 0.5 * (1.0 + jnp.tanh(0.5 * x))


def _silu(x):
    return x * _sigmoid(x)


def _gelu_tanh(x):
    c = math.sqrt(2.0 / math.pi)
    return x * (0.5 * (1.0 + jnp.tanh(c * (x + 0.044715 * (x * x * x)))))


def _dot(a, b):
    return jnp.dot(a, b, preferred_element_type=F32)


def _layernorm(x, g, b):
    mu = jnp.mean(x, axis=-1, keepdims=True)
    xc = x - mu
    var = jnp.mean(xc * xc, axis=-1, keepdims=True)
    return xc * lax.rsqrt(var + EPS) * g + b


def _ada_rmsnorm(x, g, scale, shift):
    y = x * lax.rsqrt(jnp.mean(x * x, axis=-1, keepdims=True) + EPS)
    return (y * g) * (1.0 + scale) + shift


def _params(*sem):
    return pltpu.CompilerParams(dimension_semantics=sem, vmem_limit_bytes=VMEM_LIMIT)


def _resident(shape):
    nd = len(shape)
    return pl.BlockSpec(shape, lambda *_: (0,) * nd, pipeline_mode=pl.Buffered(1))


def _ada_kernel(c_ref, w_ref, b_ref, o_ref):
    ca = _silu(c_ref[...]).astype(BF16)
    o_ref[...] = _dot(ca, w_ref[...].astype(BF16)) + b_ref[...]


def _ada_mod(c, ada_w, ada_b):
    L, D, N = ada_w.shape
    B = c.shape[0]
    out = pl.pallas_call(
        _ada_kernel,
        grid=(L, N // D),
        in_specs=[
            pl.BlockSpec((B, D), lambda l, j: (0, 0)),
            pl.BlockSpec((None, D, D), lambda l, j: (l, 0, j)),
            pl.BlockSpec((None, 1, D), lambda l, j: (l, 0, j)),
        ],
        out_specs=pl.BlockSpec((None, B, D), lambda l, j: (l, 0, j)),
        out_shape=jax.ShapeDtypeStruct((L, B, N), F32),
        compiler_params=_params("arbitrary", "arbitrary"),
        name="ada_mod",
    )(c, ada_w, ada_b.reshape(L, 1, N))
    return out.reshape(L, B, N_MOD, D)


def _mixer_in_kernel(x_ref, mod_ref, ng_ref, w_ref, b_ref, lng_ref, lnb_ref,
                     ua_ref, p_ref, su_ref, sv_ref, g0_ref, g1_ref, g2_ref):
    D = D_MODEL
    h = _ada_rmsnorm(x_ref[...], ng_ref[...], mod_ref[0, 1:2, :], mod_ref[0, 0:1, :])
    hb = h.astype(BF16)

    def proj(j):
        return _dot(hb, w_ref[:, j * D:(j + 1) * D]) + b_ref[:, j * D:(j + 1) * D]

    ua_ref[...] = (proj(0) * _sigmoid(proj(1))).astype(BF16)
    p_ref[...] = proj(2).astype(BF16)
    su_ref[...] = _gelu_tanh(proj(3)).astype(BF16)
    sv_ref[...] = _layernorm(_gelu_tanh(proj(4)), lng_ref[...], lnb_ref[...]).astype(BF16)
    g0_ref[...] = _sigmoid(proj(5)).astype(BF16)
    g1_ref[...] = _sigmoid(proj(6)).astype(BF16)
    g2_ref[...] = _sigmoid(proj(7)).astype(BF16)


def _mixer_in(x2, mod, norm_g, w_in, b_in, sgu_ln_g, sgu_ln_b, seq):
    T, D = x2.shape
    tm = TM_IN
    tiles_per_seq = seq // tm
    row = pl.BlockSpec((tm, D), lambda i: (i, 0))
    outs = pl.pallas_call(
        _mixer_in_kernel,
        grid=(T // tm,),
        in_specs=[
            row,
            pl.BlockSpec((1, N_MOD, D), lambda i: (i // tiles_per_seq, 0, 0)),
            _resident((1, D)),
            _resident(w_in.shape),
            _resident((1, w_in.shape[1])),
            _resident((1, D)),
            _resident((1, D)),
        ],
        out_specs=[row] * 7,
        out_shape=[jax.ShapeDtypeStruct((T, D), BF16)] * 7,
        compiler_params=_params("arbitrary"),
        name="mixer_in",
    )(x2, mod, norm_g.reshape(1, D), w_in, b_in.reshape(1, -1),
      sgu_ln_g.reshape(1, D), sgu_ln_b.reshape(1, D))
    return outs


def _mixer_mid_kernel(x_ref, mod_ref, ua_ref, p_ref, su_ref, sv_ref, g0_ref, g1_ref, g2_ref,
                      cw_ref, cb_ref, clg_ref, clb_ref, pw_ref, pb_ref,
                      plw_ref, plb_ref, pls_ref, sw_ref, sbt_ref, wo_ref,
                      o_ref, u_ext, p_ext, cv_ref, mg_ref, q_ext):
    D = D_MODEL
    ts = TS_MIX
    s_idx = pl.program_id(1)

    @pl.when(s_idx == 0)
    def _():
        u_ext[0:CONV_HALO, :] = jnp.zeros((CONV_HALO, D), F32)
        p_ext[0:POOL_HALO, :] = jnp.zeros((POOL_HALO, D), F32)

    @pl.when(s_idx > 0)
    def _():
        u_ext[0:CONV_HALO, :] = u_ext[ts:ts + CONV_HALO, :]
        p_ext[0:POOL_HALO, :] = p_ext[ts:ts + POOL_HALO, :]

    u_ext[CONV_HALO:CONV_HALO + ts, :] = ua_ref[...].astype(F32)
    p_ext[POOL_HALO:POOL_HALO + ts, :] = p_ref[...].astype(F32)

    ri = lax.broadcasted_iota(jnp.int32, (SGU_CHUNK, SGU_CHUNK), 0)
    ci = lax.broadcasted_iota(jnp.int32, (SGU_CHUNK, SGU_CHUNK), 1)
    causal = ri >= ci
    for hd in range(N_SGU_HEADS):
        cols = slice(hd * SGU_CHUNK, (hd + 1) * SGU_CHUNK)
        wm = jnp.where(causal, sw_ref[hd], 0.0).astype(BF16)
        bcol = sbt_ref[:, hd:hd + 1]
        for n in range(ts // SGU_CHUNK):
            rows = slice(n * SGU_CHUNK, (n + 1) * SGU_CHUNK)
            mixed = _dot(wm, sv_ref[rows, cols]) + bcol
            mg_ref[rows, cols] = (g2_ref[rows, cols].astype(F32)
                                  * su_ref[rows, cols].astype(F32) * mixed)

    pos1 = (s_idx * ts + lax.broadcasted_iota(jnp.int32, (ts, 1), 0) + 1).astype(F32)
    for gi, w in enumerate(POOL_WINDOWS):
        cols = slice(gi * POOL_GROUP, (gi + 1) * POOL_GROUP)
        cur = p_ext[POOL_HALO:POOL_HALO + ts, cols]
        read = lambda lo, n, cols=cols: p_ext[lo:lo + n, cols]
        k, lo = 1, SUBLANES
        while True:
            n = ts + POOL_HALO - lo
            nxt = read(lo, n) + read(lo - k, n)
            k *= 2
            if k == w:
                win = nxt[POOL_HALO - lo:, :]
                break
            lvl = q_ext.at[(lo // SUBLANES) % 2]
            lvl[lo:lo + n, :] = nxt
            read = lambda lo, n, lvl=lvl: lvl[lo:lo + n, :]
            lo += SUBLANES
        d = win * (1.0 / jnp.minimum(pos1, float(w))) - cur
        yb = (_dot(d.astype(BF16), plw_ref[gi]) + plb_ref[:, cols]) * pls_ref[:, cols]
        mg_ref[:, cols] += g1_ref[:, cols].astype(F32) * yb

    rc = CONV_ROWS
    n_a = -(-CONV_WIDTH // SUBLANES)

    def conv_rows(ci, carry):
        r0 = pl.multiple_of(ci * rc, rc)
        for ct in range(D // LANES):
            cols = slice(ct * LANES, (ct + 1) * LANES)
            acc = jnp.zeros((rc, LANES), F32) + cb_ref[:, cols]
            for r in range(SUBLANES):
                v = None
                for a in range(n_a):
                    j = SUBLANES * a + r
                    if j >= CONV_WIDTH:
                        continue
                    start = r0 + (CONV_HALO - SUBLANES - SUBLANES * a)
                    term = (u_ext[pl.ds(start, rc + SUBLANES), cols]
                            * cw_ref[CONV_WIDTH - 1 - j:CONV_WIDTH - j, cols])
                    v = term if v is None else v + term
                acc = acc + v[SUBLANES - r:SUBLANES - r + rc, :]
            cv_ref[pl.ds(r0, rc), cols] = acc
        return carry

    lax.fori_loop(0, ts // rc, conv_rows, 0)
    cn = _silu(_layernorm(cv_ref[...], clg_ref[...], clb_ref[...])).astype(BF16)
    ya = _dot(cn, pw_ref[...]) + pb_ref[...]
    merged = mg_ref[...] + g0_ref[...].astype(F32) * ya

    y = _dot(merged.astype(BF16), wo_ref[...])
    o_ref[...] = x_ref[...] + mod_ref[0, 2:3, :] * y


def _mixer_mid(x2, mod, parts, conv_w, conv_b, conv_ln_g, conv_ln_b, conv_pw_w, conv_pw_b,
               pool_w, pool_b, pool_scale, sgu_w, sgu_b, w_out, batch, seq):
    T, D = x2.shape
    ts = TS_MIX
    ns = seq // ts
    row = pl.BlockSpec((ts, D), lambda b, s: (b * ns + s, 0))
    vec = lambda a: a.reshape(1, D)
    return pl.pallas_call(
        _mixer_mid_kernel,
        grid=(batch, ns),
        in_specs=[
            row,
            pl.BlockSpec((1, N_MOD, D), lambda b, s: (b, 0, 0)),
        ] + [row] * 7 + [
            _resident((CONV_WIDTH, D)), _resident((1, D)), _resident((1, D)), _resident((1, D)),
            _resident((D, D)), _resident((1, D)),
            _resident(pool_w.shape), _resident((1, D)), _resident((1, D)),
            _resident(sgu_w.shape), _resident((SGU_CHUNK, N_SGU_HEADS)),
            _resident((D, D)),
        ],
        out_specs=row,
        out_shape=jax.ShapeDtypeStruct((T, D), F32),
        scratch_shapes=[
            pltpu.VMEM((ts + CONV_HALO, D), F32),
            pltpu.VMEM((ts + POOL_HALO, D), F32),
            pltpu.VMEM((ts, D), F32),
            pltpu.VMEM((ts, D), F32),
            pltpu.VMEM((2, ts + POOL_HALO, POOL_GROUP), F32),
        ],
        compiler_params=_params("arbitrary", "arbitrary"),
        name="mixer_mid",
    )(x2, mod, *parts, conv_w, vec(conv_b), vec(conv_ln_g), vec(conv_ln_b), conv_pw_w,
      vec(conv_pw_b), pool_w, vec(pool_b), vec(pool_scale), sgu_w, sgu_b.T, w_out)


def _ffn_kernel(x_ref, mod_ref, xn_ref, modn_ref, ng_ref, w1_ref, w3_ref, w2_ref, o_ref, h_ref):
    def normed(xr, mr):
        return _ada_rmsnorm(xr[...], ng_ref[...], mr[0, 4:5, :], mr[0, 3:4, :]).astype(BF16)

    @pl.when(pl.program_id(0) == 0)
    def _():
        h_ref[...] = normed(x_ref, mod_ref)

    hb = h_ref[...]
    act = (_silu(_dot(hb, w1_ref[...])) * _dot(hb, w3_ref[...])).astype(BF16)
    o_ref[...] = x_ref[...] + mod_ref[0, 5:6, :] * _dot(act, w2_ref[...])
    h_ref[...] = normed(xn_ref, modn_ref)


def _ffn(x2, mod, norm_g, w1, w3, w2, seq):
    T, D = x2.shape
    tm = TM_FFN
    tiles_per_seq = seq // tm
    n_tiles = T // tm
    nxt = lambda i: jnp.minimum(i + 1, n_tiles - 1)
    row = pl.BlockSpec((tm, D), lambda i: (i, 0))
    return pl.pallas_call(
        _ffn_kernel,
        grid=(n_tiles,),
        in_specs=[
            row,
            pl.BlockSpec((1, N_MOD, D), lambda i: (i // tiles_per_seq, 0, 0)),
            pl.BlockSpec((tm, D), lambda i: (nxt(i), 0)),
            pl.BlockSpec((1, N_MOD, D), lambda i: (nxt(i) // tiles_per_seq, 0, 0)),
            _resident((1, D)),
            _resident(w1.shape),
            _resident(w3.shape),
            _resident(w2.shape),
        ],
        out_specs=row,
        out_shape=jax.ShapeDtypeStruct((T, D), F32),
        scratch_shapes=[pltpu.VMEM((tm, D), BF16)],
        compiler_params=_params("arbitrary"),
        name="ffn",
    )(x2, mod, x2, mod, norm_g.reshape(1, D), w1, w3, w2)


def _split_bf16(v):
    hi = v.astype(BF16)
    lo = (v - hi.astype(F32)).astype(BF16)
    return hi, lo


def _router_kernel(x_ref, mod_ref, ng_ref, rw_ref, h_ref, idx_ref, gate_ref):
    tm = x_ref.shape[0]
    h = _ada_rmsnorm(x_ref[...], ng_ref[...], mod_ref[0, 4:5, :], mod_ref[0, 3:4, :])
    for s in range(ROW_TILES):
        h_ref[pl.ds(s, tm, stride=ROW_TILES), :] = h[:, s * LANES:(s + 1) * LANES]

    h_hi, h_lo = _split_bf16(h)
    w_hi, w_lo = _split_bf16(rw_ref[...])
    logits = _dot(h_hi, w_hi) + (_dot(h_lo, w_hi) + _dot(h_hi, w_lo))

    lane = lax.broadcasted_iota(jnp.int32, (tm, LANES), 1)
    neg = jnp.float32(-jnp.inf)
    logits = jnp.where(lane < N_EXPERTS, logits, neg)
    m1 = jnp.max(logits, axis=-1, keepdims=True)
    i1 = jnp.min(jnp.where(logits == m1, lane, LANES), axis=-1, keepdims=True)
    rest = jnp.where(lane == i1, neg, logits)
    m2 = jnp.max(rest, axis=-1, keepdims=True)
    i2 = jnp.min(jnp.where(rest == m2, lane, LANES), axis=-1, keepdims=True)
    e2 = jnp.exp(m2 - m1)
    den = 1.0 + e2
    idx_ref[...] = jnp.where(lane == 0, i1, i2)[:, :TOP_K]
    gate_ref[...] = jnp.where(lane == 0, 1.0 / den, e2 / den)[:, :TOP_K]


def _router(x2, mod, norm_g, router_w, seq):
    T, D = x2.shape
    tm = TM_ROUTE
    tiles_per_seq = seq // tm
    rw = jnp.zeros((D, LANES), F32).at[:, :N_EXPERTS].set(router_w)
    return pl.pallas_call(
        _router_kernel,
        grid=(T // tm,),
        in_specs=[
            pl.BlockSpec((tm, D), lambda i: (i, 0)),
            pl.BlockSpec((1, N_MOD, D), lambda i: (i // tiles_per_seq, 0, 0)),
            _resident((1, D)),
            _resident((D, LANES)),
        ],
        out_specs=[
            pl.BlockSpec((tm * ROW_TILES, LANES), lambda i: (i, 0)),
            pl.BlockSpec((tm, TOP_K), lambda i: (i, 0)),
            pl.BlockSpec((tm, TOP_K), lambda i: (i, 0)),
        ],
        out_shape=[
            jax.ShapeDtypeStruct((T * ROW_TILES, LANES), F32),
            jax.ShapeDtypeStruct((T, TOP_K), jnp.int32),
            jax.ShapeDtypeStruct((T, TOP_K), F32),
        ],
        compiler_params=_params("arbitrary"),
        name="router",
    )(x2, mod, norm_g.reshape(1, D), rw)


def _dispatch_plan(top_idx, n_tokens):
    bm = BM_MOE
    n_flat = n_tokens * TOP_K
    n_blocks = -(-(n_flat + N_EXPERTS * (bm - 1)) // bm)
    n_slots = n_blocks * bm
    flat_e = top_idx.reshape(-1)
    order = jnp.argsort(flat_e, stable=True).astype(jnp.int32)
    counts = jnp.sum(flat_e[:, None] == jnp.arange(N_EXPERTS, dtype=jnp.int32)[None, :],
                     axis=0, dtype=jnp.int32)
    padded = (counts + bm - 1) // bm * bm
    start_sorted = jnp.cumsum(counts) - counts
    ends_padded = jnp.cumsum(padded)
    start_padded = ends_padded - padded
    n_used = (ends_padded[-1] // bm).astype(jnp.int32)
    block_start = jnp.arange(n_blocks, dtype=jnp.int32) * bm
    block_expert = jnp.minimum(jnp.searchsorted(ends_padded, block_start, side="right"),
                               N_EXPERTS - 1).astype(jnp.int32)
    last_e = block_expert[jnp.maximum(n_used - 1, 0)]
    block_expert = jnp.where(jnp.arange(n_blocks) < n_used, block_expert, last_e)
    slot = jnp.arange(n_slots, dtype=jnp.int32)
    slot_e = jnp.repeat(block_expert, bm)
    rank = slot - start_padded[slot_e]
    valid = (rank < counts[slot_e]) & (slot < ends_padded[-1])
    flat = order[jnp.clip(start_sorted[slot_e] + rank, 0, n_flat - 1)]
    spare = n_flat + jnp.cumsum(jnp.logical_not(valid).astype(jnp.int32)) - 1
    slot_src = jnp.where(valid, flat // TOP_K, 0).astype(jnp.int32)
    slot_dst = jnp.where(valid, (flat % TOP_K) * n_tokens + flat // TOP_K, spare).astype(jnp.int32)
    return (slot_src.reshape(n_blocks, 1, bm), slot_dst.reshape(n_blocks, 1, bm),
            block_expert, n_used.reshape(1), n_blocks)


def _moe_kernel(nf, be_ref, nu_ref, src_cur, src_nxt, dst_cur, dst_prv, h_hbm, w1_ref, w3_ref,
                w2_ref, y_hbm, xbuf, xs_ref, acc_ref, ybuf, gsem, ssem):
    bm = BM_MOE
    b = pl.program_id(0)
    f = pl.program_id(1)
    n_used = nu_ref[0]
    slot = b % 2

    rt = ROW_TILES
    blk_rows = bm * rt
    per_step = bm // nf

    def token_rows(ref, tok):
        return ref.at[pl.ds(pl.multiple_of(tok * rt, rt), rt), :]

    def gather_copy(idx_ref, buf_slot, r):
        return pltpu.make_async_copy(token_rows(h_hbm, idx_ref[0, 0, r]),
                                     token_rows(xbuf, buf_slot * bm + r), gsem.at[buf_slot])

    def scatter_copy(idx_ref, r):
        return pltpu.make_async_copy(token_rows(ybuf, r), token_rows(y_hbm, idx_ref[0, 0, r]),
                                     ssem.at[0])

    def wait_gather(buf_slot):
        pltpu.make_async_copy(h_hbm.at[pl.ds(0, blk_rows), :],
                              xbuf.at[pl.ds(pl.multiple_of(buf_slot * blk_rows, blk_rows), blk_rows), :],
                              gsem.at[buf_slot]).wait()

    def wait_scatter():
        pltpu.make_async_copy(ybuf, y_hbm.at[pl.ds(0, blk_rows), :], ssem.at[0]).wait()

    def issue_piece(piece, n_pieces):
        lo = per_step * piece // n_pieces
        hi = per_step * (piece + 1) // n_pieces
        for k in range(lo, hi):
            r = f * per_step + k
            gather_copy(src_nxt, 1 - slot, r).start()
            scatter_copy(dst_prv, r).start(priority=k % 2)

    @pl.when(b < n_used)
    def _():
        @pl.when(f == 0)
        def _():
            @pl.when(b == 0)
            def _():
                def body(r, carry):
                    gather_copy(src_cur, 0, r).start()
                    return carry
                lax.fori_loop(0, bm, body, 0, unroll=DMA_UNROLL)
                ybuf[...] = jnp.zeros_like(ybuf)

            wait_gather(slot)
            base = slot * blk_rows
            for s in range(rt):
                xs_ref[:, s * LANES:(s + 1) * LANES] = (
                    xbuf[pl.ds(base + s, bm, stride=rt), :].astype(BF16))
            acc_ref[...] = jnp.zeros_like(acc_ref)

        xb = xs_ref[...]
        issue_piece(0, 3)
        h1 = _dot(xb, w1_ref[...])
        issue_piece(1, 3)
        h3 = _dot(xb, w3_ref[...])
        issue_piece(2, 3)
        act = (_silu(h1) * h3).astype(BF16)
        acc_ref[...] += _dot(act, w2_ref[...])

        @pl.when(f == nf - 1)
        def _():
            wait_scatter()
            for s in range(rt):
                ybuf[pl.ds(s, bm, stride=rt), :] = acc_ref[:, s * LANES:(s + 1) * LANES]

            @pl.when(b == n_used - 1)
            def _():
                def body(r, carry):
                    scatter_copy(dst_cur, r).start()
                    return carry
                lax.fori_loop(0, bm, body, 0, unroll=DMA_UNROLL)
                wait_scatter()
                wait_gather(1 - slot)

    @pl.when(jnp.logical_and(b >= n_used, f == pl.num_programs(1) - 1))
    def _():
        ybuf[...] = jnp.zeros_like(ybuf)
        fill = pltpu.make_async_copy(
            ybuf, y_hbm.at[pl.ds(pl.multiple_of(b * blk_rows, blk_rows), blk_rows), :], ssem.at[0])
        fill.start()
        fill.wait()


def _moe_experts(h3, slot_src, slot_dst, block_expert, n_used, n_blocks, w1, w3, w2):
    D = D_MODEL
    bm, tf = BM_MOE, TF_MOE
    F = w1.shape[2]
    nf = F // tf
    n_slots = n_blocks * bm

    def fsel(b, f, nu):
        return jnp.where(b < nu[0], f, nf - 1)

    smem_blk = lambda imap: pl.BlockSpec((1, 1, bm), imap, memory_space=pltpu.SMEM)
    grid_spec = pltpu.PrefetchScalarGridSpec(
        num_scalar_prefetch=2,
        grid=(n_blocks, nf),
        in_specs=[
            smem_blk(lambda b, f, be, nu: (b, 0, 0)),
            smem_blk(lambda b, f, be, nu: (jnp.minimum(b + 1, n_blocks - 1), 0, 0)),
            smem_blk(lambda b, f, be, nu: (b, 0, 0)),
            smem_blk(lambda b, f, be, nu: (jnp.maximum(b - 1, 0), 0, 0)),
            pl.BlockSpec(memory_space=pl.ANY),
            pl.BlockSpec((None, D, tf), lambda b, f, be, nu: (be[b], 0, fsel(b, f, nu))),
            pl.BlockSpec((None, D, tf), lambda b, f, be, nu: (be[b], 0, fsel(b, f, nu))),
            pl.BlockSpec((None, tf, D), lambda b, f, be, nu: (be[b], fsel(b, f, nu), 0)),
        ],
        out_specs=pl.BlockSpec(memory_space=pl.ANY),
        scratch_shapes=[
            pltpu.VMEM((2 * bm * ROW_TILES, LANES), F32),
            pltpu.VMEM((bm, D), BF16),
            pltpu.VMEM((bm, D), F32),
            pltpu.VMEM((bm * ROW_TILES, LANES), F32),
            pltpu.SemaphoreType.DMA((2,)),
            pltpu.SemaphoreType.DMA((1,)),
        ],
    )
    return pl.pallas_call(
        functools.partial(_moe_kernel, nf),
        grid_spec=grid_spec,
        out_shape=jax.ShapeDtypeStruct((n_slots * ROW_TILES, LANES), F32),
        compiler_params=_params("arbitrary", "arbitrary"),
        name="moe_experts",
    )(block_expert, n_used, slot_src, slot_src, slot_dst, slot_dst, h3, w1, w3, w2)


def _combine_kernel(x_ref, mod_ref, y0_ref, y1_ref, gate_ref, fg_ref, o_ref, f_ref):
    tm = x_ref.shape[0]
    g = gate_ref[...]
    for s in range(ROW_TILES):
        rows = pl.ds(s, tm, stride=ROW_TILES)
        f_ref[:, s * LANES:(s + 1) * LANES] = g[:, 0:1] * y0_ref[rows, :] + g[:, 1:2] * y1_ref[rows, :]
    x = x_ref[...] + mod_ref[0, 5:6, :] * f_ref[...]
    o_ref[...] = (x * lax.rsqrt(jnp.mean(x * x, axis=-1, keepdims=True) + EPS)) * fg_ref[...]


def _combine(x2, mod, y_slots, gates, final_g, seq):
    T, D = x2.shape
    tm = TM_OUT
    tiles_per_seq = seq // tm
    n_tiles = T // tm
    row = pl.BlockSpec((tm, D), lambda i: (i, 0))
    return pl.pallas_call(
        _combine_kernel,
        grid=(T // tm,),
        in_specs=[
            row,
            pl.BlockSpec((1, N_MOD, D), lambda i: (i // tiles_per_seq, 0, 0)),
            pl.BlockSpec((tm * ROW_TILES, LANES), lambda i: (i, 0)),
            pl.BlockSpec((tm * ROW_TILES, LANES), lambda i: (n_tiles + i, 0)),
            pl.BlockSpec((tm, TOP_K), lambda i: (i, 0)),
            _resident((1, D)),
        ],
        out_specs=row,
        out_shape=jax.ShapeDtypeStruct((T, D), F32),
        scratch_shapes=[pltpu.VMEM((tm, D), F32)],
        compiler_params=_params("arbitrary"),
        name="combine_norm",
    )(x2, mod, y_slots, y_slots, gates, final_g.reshape(1, D))


def _mixer_layer(x2, mod, i, batch, seq, p):
    parts = _mixer_in(x2, mod, p["norm1_g"][i], p["w_in"][i].astype(BF16), p["b_in"][i],
                      p["sgu_ln_g"][i], p["sgu_ln_b"][i], seq)
    return _mixer_mid(x2, mod, parts, p["conv_w"][i], p["conv_b"][i], p["conv_ln_g"][i],
                      p["conv_ln_b"][i], p["conv_pw_w"][i].astype(BF16), p["conv_pw_b"][i],
                      p["pool_w"][i].astype(BF16), p["pool_b"][i], p["pool_scale"][i],
                      p["sgu_w"][i], p["sgu_b"][i], p["w_out"][i].astype(BF16), batch, seq)


def kernel(x, c, ada_w, ada_b, norm1_g, norm2_g, w_in, b_in, conv_w, conv_b, conv_ln_g, conv_ln_b, conv_pw_w, conv_pw_b, pool_w, pool_b, pool_scale, sgu_ln_g, sgu_ln_b, sgu_w, sgu_b, w_out, ffn_w1, ffn_w3, ffn_w2, router_w, moe_w1, moe_w3, moe_w2, final_norm_g):
    p = dict(norm1_g=norm1_g, w_in=w_in, b_in=b_in, conv_w=conv_w, conv_b=conv_b,
             conv_ln_g=conv_ln_g, conv_ln_b=conv_ln_b, conv_pw_w=conv_pw_w, conv_pw_b=conv_pw_b,
             pool_w=pool_w, pool_b=pool_b, pool_scale=pool_scale, sgu_ln_g=sgu_ln_g,
             sgu_ln_b=sgu_ln_b, sgu_w=sgu_w, sgu_b=sgu_b, w_out=w_out)
    batch, seq, D = x.shape
    T = batch * seq
    mod = _ada_mod(c, ada_w, ada_b)
    x2 = x.reshape(T, D)

    x2 = _mixer_layer(x2, mod[0], 0, batch, seq, p)
    x2 = _ffn(x2, mod[0], norm2_g[0], ffn_w1[0].astype(BF16), ffn_w3[0].astype(BF16),
              ffn_w2[0].astype(BF16), seq)

    x2 = _mixer_layer(x2, mod[1], 1, batch, seq, p)
    h3, top_idx, gates = _router(x2, mod[1], norm2_g[1], router_w[0], seq)
    slot_src, slot_dst, block_expert, n_used, n_blocks = _dispatch_plan(top_idx, T)
    y_slots = _moe_experts(h3, slot_src, slot_dst, block_expert, n_used, n_blocks,
                           moe_w1[0].astype(BF16), moe_w3[0].astype(BF16), moe_w2[0].astype(BF16))
    out = _combine(x2, mod[1], y_slots, gates, final_norm_g, seq)
    return out.reshape(batch, seq, D)
```
